```python
import math
import jax, jax.numpy as jnp
from jax import lax
import numpy as np

D_MODEL = 1024
BATCH = 8
SEQ = 4096
DEPTH = 1

D_MIX = 2 * D_MODEL
D_ATT = D_MIX // 2
ATT_HEAD_DIM = 64
ATT_HEADS = D_ATT // ATT_HEAD_DIM
ATT_STEPS = 128
DILATIONS = (1, 4, 16)
D_SSM = D_MIX - D_ATT
SSM_HEAD_DIM = 64
SSM_HEADS = D_SSM // SSM_HEAD_DIM
SSM_GROUPS = 2
SSM_STATE = 128
CONV_WIDTH = 4
CHUNK = 128
D_CONV = D_SSM + 2 * SSM_GROUPS * SSM_STATE
D_IN_PROJ = 4 * D_ATT + D_SSM + D_CONV + SSM_HEADS
SPLITS = (D_ATT, 2 * D_ATT, 3 * D_ATT, 4 * D_ATT, 4 * D_ATT + D_SSM, 4 * D_ATT + D_SSM + D_CONV)
NORM_EPS = 1e-5

kernel_name = 'hybrid_dilated_attn_ssd_block'


def rms_norm(u, g):
    uf = u.astype(jnp.float32)
    uf = uf * lax.rsqrt(jnp.mean(uf * uf, axis=-1, keepdims=True) + NORM_EPS)
    return (uf * g.astype(jnp.float32)).astype(u.dtype)


def layer_norm(u, g, b):
    uf = u.astype(jnp.float32)
    mu = jnp.mean(uf, axis=-1, keepdims=True)
    var = jnp.mean(jnp.square(uf - mu), axis=-1, keepdims=True)
    out = (uf - mu) * lax.rsqrt(var + NORM_EPS) * g.astype(jnp.float32) + b.astype(jnp.float32)
    return out.astype(u.dtype)


def alibi_slopes(n_heads):
    return jnp.asarray(2.0 ** (-8.0 * np.arange(1, n_heads + 1) / n_heads), dtype=jnp.float32)


def dilated_window_attention(q, k, v, slopes, dilation):
    B, S, H, E = q.shape
    unit = dilation * ATT_STEPS
    s_pad = -(-S // unit) * unit
    nb = s_pad // unit
    pad = ((0, 0), (0, s_pad - S), (0, 0), (0, 0))

    def to_blocks(t):
        return jnp.pad(t, pad).reshape(B, nb, ATT_STEPS, dilation, H, E)

    def with_prev(t):
        prev = jnp.pad(t, ((0, 0), (1, 0), (0, 0), (0, 0), (0, 0), (0, 0)))[:, :-1]
        return jnp.concatenate([prev, t], axis=2)

    qb = to_blocks(q)
    kk = with_prev(to_blocks(k))
    vv = with_prev(to_blocks(v))
    scores = jnp.einsum('bnqrhe,bnkrhe->bnrhqk', qb, kk).astype(jnp.float32)

    n_idx = jnp.arange(nb)[:, None, None]
    jq = n_idx * ATT_STEPS + jnp.arange(ATT_STEPS)[None, :, None]
    jk = (n_idx - 1) * ATT_STEPS + jnp.arange(2 * ATT_STEPS)[None, None, :]
    steps = jq - jk
    valid = (jk >= 0) & (steps >= 0) & (steps <= ATT_STEPS)
    dist = (steps * dilation).astype(jnp.float32)
    bias = -slopes[None, None, :, None, None] * dist[:, None, None, :, :]
    scores = jnp.where(valid[:, None, None], scores + bias, -jnp.inf)

    lse = jax.nn.logsumexp(scores, axis=-1)
    p = jnp.exp(scores - lse[..., None]).astype(vv.dtype)
    o = jnp.einsum('bnrhqk,bnkrhe->bnqrhe', p, vv)
    o = o.reshape(B, s_pad, H, E)[:, :S]
    lse = lse.transpose(0, 1, 4, 2, 3).reshape(B, s_pad, H)[:, :S]
    return o, lse


def causal_depthwise_conv(u, w, bias):
    out = lax.conv_general_dilated(
        u, w[:, None, :].astype(u.dtype), window_strides=(1,), padding=((CONV_WIDTH - 1, 0),),
        dimension_numbers=('NWC', 'WIO', 'NWC'), feature_group_count=u.shape[-1])
    return out + bias.astype(u.dtype)


def segsum(a):
    cs = jnp.cumsum(a, axis=-1)
    l = a.shape[-1]
    diff = cs[..., :, None] - cs[..., None, :]
    mask = jnp.tril(jnp.ones((l, l), dtype=bool))
    return jnp.where(mask, diff, -jnp.inf)


def ssd_chunked(x, dt, a, b, c):
    Bsz, S, H, P = x.shape
    G, N = b.shape[2], b.shape[3]
    J = H // G
    nc = S // CHUNK
    xg = (x * dt[..., None]).reshape(Bsz, nc, CHUNK, G, J, P)
    adt = (dt * a).reshape(Bsz, nc, CHUNK, G, J).transpose(0, 1, 3, 4, 2)
    bc = b.reshape(Bsz, nc, CHUNK, G, N)
    cc = c.reshape(Bsz, nc, CHUNK, G, N)
    a_cs = jnp.cumsum(adt, axis=-1)

    decay = jnp.exp(segsum(adt))
    cb = jnp.einsum('bclgn,bcsgn->bcgls', cc, bc)
    y_diag = jnp.einsum('bcgjls,bcsgjp->bclgjp', cb[:, :, :, None] * decay, xg)

    decay_to_end = jnp.exp(a_cs[..., -1:] - a_cs).transpose(0, 1, 4, 2, 3)[..., None]
    states = jnp.einsum('bcsgn,bcsgjp->bcgjpn', bc, xg * decay_to_end)

    chunk_decay = jnp.exp(a_cs[..., -1])

    def step(h, inp):
        st, dec = inp
        return h * dec[..., None, None] + st, h

    h0 = jnp.zeros((Bsz, G, J, P, N), dtype=jnp.float32)
    _, prev = lax.scan(step, h0, (jnp.moveaxis(states, 1, 0), jnp.moveaxis(chunk_decay, 1, 0)))
    prev = jnp.moveaxis(prev, 0, 1)

    state_decay = jnp.exp(a_cs).transpose(0, 1, 4, 2, 3)[..., None]
    y_off = jnp.einsum('bclgn,bcgjpn->bclgjp', cc, prev) * state_decay
    return (y_diag + y_off).reshape(Bsz, S, H, P)


def setup_inputs(seed: int = 0) -> dict:
    key = jax.random.key(seed)
    ks = jax.random.split(key, 14)
    beta = (8.0 * DEPTH) ** -0.25
    f32 = jnp.float32
    x = jax.random.normal(ks[0], (BATCH, SEQ, D_MODEL), f32)
    col_scale = np.ones((D_IN_PROJ,), np.float32)
    col_scale[2 * D_ATT:3 * D_ATT] = beta
    col_scale[4 * D_ATT + D_SSM:4 * D_ATT + 2 * D_SSM] = beta
    w_in = jax.random.normal(ks[1], (DEPTH, D_MODEL, D_IN_PROJ), f32) * (D_MODEL ** -0.5) * jnp.asarray(col_scale)
    conv_w = jax.random.normal(ks[2], (DEPTH, CONV_WIDTH, D_CONV), f32) * (CONV_WIDTH ** -0.5)
    conv_b = 0.02 * jax.random.normal(ks[3], (DEPTH, D_CONV), f32)
    dt0 = jnp.exp(jax.random.uniform(ks[4], (DEPTH, SSM_HEADS), f32, math.log(1e-3), math.log(1e-1)))
    dt_bias = dt0 + jnp.log(-jnp.expm1(-dt0))
    a_log = jnp.log(jax.random.uniform(ks[5], (DEPTH, SSM_HEADS), f32, 1.0, 16.0))
    d_skip = 1.0 + 0.1 * jax.random.normal(ks[6], (DEPTH, SSM_HEADS), f32)
    att_norm_g = 1.0 + 0.1 * jax.random.normal(ks[7], (DEPTH, D_ATT), f32)
    ssm_norm_g = 1.0 + 0.1 * jax.random.normal(ks[8], (DEPTH, D_SSM), f32)
    w_out = jax.random.normal(ks[9], (DEPTH, D_MIX, D_MODEL), f32) * (D_MIX ** -0.5) * beta
    ln_g = 1.0 + 0.1 * jax.random.normal(ks[10], (DEPTH, D_MODEL), f32)
    ln_b = 0.02 * jax.random.normal(ks[11], (DEPTH, D_MODEL), f32)
    return {'x': x, 'w_in': w_in, 'conv_w': conv_w, 'conv_b': conv_b, 'dt_bias': dt_bias,
            'a_log': a_log, 'd_skip': d_skip, 'att_norm_g': att_norm_g, 'ssm_norm_g': ssm_norm_g,
            'w_out': w_out, 'ln_g': ln_g, 'ln_b': ln_b}


def reference(x, w_in, conv_w, conv_b, dt_bias, a_log, d_skip, att_norm_g, ssm_norm_g, w_out, ln_g, ln_b):
    alpha = (2.0 * DEPTH) ** 0.25
    slopes = alibi_slopes(ATT_HEADS)
    Bsz, S, _ = x.shape
    for layer in range(DEPTH):
        proj = jnp.einsum('bsd,de->bse', x, w_in[layer])
        q, k, v, z_att, z_ssm, xbc, dt_raw = jnp.split(proj, list(SPLITS), axis=-1)

        q = q.reshape(Bsz, S, ATT_HEADS, ATT_HEAD_DIM) * (ATT_HEAD_DIM ** -0.5)
        k = k.reshape(Bsz, S, ATT_HEADS, ATT_HEAD_DIM)
        v = v.reshape(Bsz, S, ATT_HEADS, ATT_HEAD_DIM)
        o1, l1 = dilated_window_attention(q, k, v, slopes, DILATIONS[0])
        o2, l2 = dilated_window_attention(q, k, v, slopes, DILATIONS[1])
        o3, l3 = dilated_window_attention(q, k, v, slopes, DILATIONS[2])
        wts = jax.nn.softmax(jnp.stack([l1, l2, l3], axis=0), axis=0)
        y_att = (wts[0][..., None] * o1 + wts[1][..., None] * o2 + wts[2][..., None] * o3).astype(x.dtype)
        y_att = rms_norm(y_att.reshape(Bsz, S, D_ATT) * jax.nn.silu(z_att), att_norm_g[layer])

        xbc = jax.nn.silu(causal_depthwise_conv(xbc, conv_w[layer], conv_b[layer]))
        xs, bm, cm = jnp.split(xbc, [D_SSM, D_SSM + SSM_GROUPS * SSM_STATE], axis=-1)
        dt = jax.nn.softplus(dt_raw.astype(jnp.float32) + dt_bias[layer].astype(jnp.float32))
        a = -jnp.exp(a_log[layer].astype(jnp.float32))
        xs_h = xs.reshape(Bsz, S, SSM_HEADS, SSM_HEAD_DIM).astype(jnp.float32)
        y = ssd_chunked(xs_h, dt, a,
                        bm.reshape(Bsz, S, SSM_GROUPS, SSM_STATE).astype(jnp.float32),
                        cm.reshape(Bsz, S, SSM_GROUPS, SSM_STATE).astype(jnp.float32))
        y = y + d_skip[layer].astype(jnp.float32)[:, None] * xs_h
        y_ssm = rms_norm(y.reshape(Bsz, S, D_SSM).astype(x.dtype) * jax.nn.silu(z_ssm), ssm_norm_g[layer])

        mix = jnp.concatenate([y_att, y_ssm], axis=-1)
        out = jnp.einsum('bse,ed->bsd', mix, w_out[layer])
        x = layer_norm(alpha * x + out, ln_g[layer], ln_b[layer])
    return x
```

```python
import functools

import numpy as np
import jax
import jax.numpy as jnp
from jax import lax
from jax.experimental import pallas as pl
from jax.experimental.pallas import tpu as pltpu

F32 = jnp.float32
BF16 = jnp.bfloat16

D_MODEL = 1024
D_ATT = 1024
HEAD_DIM = 64
ATT_HEADS = D_ATT // HEAD_DIM
ATT_STEPS = 128
DILATIONS = (1, 4, 16)
D_SSM = 1024
SSM_HEADS = D_SSM // HEAD_DIM
SSM_GROUPS = 2
SSM_STATE = 128
CONV_WIDTH = 4
CHUNK = 128
D_BC = 2 * SSM_GROUPS * SSM_STATE
D_PROJ = 4 * D_ATT + D_SSM + D_SSM + D_BC
NORM_EPS = 1e-5

LANES = 128
HEAD_PAIRS = ATT_HEADS // 2
UNIT = DILATIONS[-1] * ATT_STEPS
NEG = -1e30
VMEM_LIMIT = 48 * 1024 * 1024

COL_Z_ATT = 3
COL_Z_SSM = 4
COL_XS = 5


def _silu(v):
    return v / (1.0 + jnp.exp(-v))


def _lane_is_first_head():
    return lax.broadcasted_iota(jnp.int32, (1, LANES), 1) < HEAD_DIM


def _in_proj_body(x_ref, w_ref, wdt_ref, proj_ref, dt_ref, xb_ref):
    @pl.when(pl.program_id(1) == 0)
    def _():
        xb = x_ref[...].astype(BF16)
        xb_ref[...] = xb
        dt_ref[...] = jnp.dot(xb, wdt_ref[...], preferred_element_type=F32)

    proj_ref[...] = jnp.dot(xb_ref[...], w_ref[...], preferred_element_type=F32).astype(BF16)


def _in_proj(x2d, w_main, w_dt):
    m = x2d.shape[0]
    tm, tn = 1024, 512
    return pl.pallas_call(
        _in_proj_body,
        grid=(m // tm, D_PROJ // tn),
        in_specs=[
            pl.BlockSpec((tm, D_MODEL), lambda i, j: (i, 0)),
            pl.BlockSpec((D_MODEL, tn), lambda i, j: (0, j)),
            pl.BlockSpec((D_MODEL, LANES), lambda i, j: (0, 0)),
        ],
        out_specs=[
            pl.BlockSpec((tm, tn), lambda i, j: (i, j)),
            pl.BlockSpec((tm, LANES), lambda i, j: (i, 0)),
        ],
        out_shape=[
            jax.ShapeDtypeStruct((m, D_PROJ), BF16),
            jax.ShapeDtypeStruct((m, LANES), F32),
        ],
        scratch_shapes=[pltpu.VMEM((tm, D_MODEL), BF16)],
        compiler_params=pltpu.CompilerParams(
            dimension_semantics=("parallel", "arbitrary"), vmem_limit_bytes=VMEM_LIMIT),
        name="in_proj",
    )(x2d, w_main, w_dt)


def _attn_body(slope_ref, q_ref, kc_ref, kp_ref, vc_ref, vp_ref, o_ref,
               qa_s, qb_s, kf, vf, acc, m_s, l_s, bias):
    hp = pl.program_id(1)
    u = pl.program_id(2)
    first_head = _lane_is_first_head()

    q = q_ref[...].astype(F32) * (HEAD_DIM ** -0.5)
    qa_s[...] = jnp.where(first_head, q, 0.0)
    qb_s[...] = jnp.where(first_head, 0.0, q)
    kf[0:UNIT, :] = kp_ref[...].astype(F32)
    kf[UNIT:2 * UNIT, :] = kc_ref[...].astype(F32)
    vf[0:UNIT, :] = vp_ref[...].astype(F32)
    vf[UNIT:2 * UNIT, :] = vc_ref[...].astype(F32)
    acc[...] = jnp.zeros_like(acc)
    l_s[...] = jnp.zeros_like(l_s)
    m_s[...] = jnp.full_like(m_s, NEG)

    row = lax.broadcasted_iota(jnp.int32, (2 * ATT_STEPS, 2 * ATT_STEPS), 0)
    col = lax.broadcasted_iota(jnp.int32, (2 * ATT_STEPS, 2 * ATT_STEPS), 1)
    step = (row & (ATT_STEPS - 1)) + ATT_STEPS - col
    valid = (step >= 0) & (step <= ATT_STEPS)
    slope = jnp.where(row < ATT_STEPS, slope_ref[2 * hp], slope_ref[2 * hp + 1])
    for di, d in enumerate(DILATIONS):
        b = jnp.where(valid, -slope * (step * d).astype(F32), NEG)
        bias[2 * di] = b
        bias[2 * di + 1] = jnp.where(col < ATT_STEPS, NEG, b)

    def substep(di, d, qs, ks, first):
        if d == 1:
            rows_q = pl.ds(qs, ATT_STEPS)
            rows_k = pl.ds(ks, 2 * ATT_STEPS)
        else:
            rows_q = pl.ds(qs, ATT_STEPS, stride=d)
            rows_k = pl.ds(ks, 2 * ATT_STEPS, stride=d)
        q2 = jnp.concatenate([qa_s[rows_q, :], qb_s[rows_q, :]], axis=0).astype(BF16)
        kk = kf[rows_k, :].astype(BF16)
        vv = vf[rows_k, :].astype(BF16)
        s = lax.dot_general(q2, kk, (((1,), (1,)), ((), ())), preferred_element_type=F32)
        s = s + bias[2 * di + first.astype(jnp.int32)]
        mb = jnp.max(s, axis=1, keepdims=True)
        p = jnp.exp(s - mb).astype(BF16)
        vext = jnp.concatenate([vv, jnp.ones_like(vv)], axis=1)
        oe = jnp.dot(p, vext, preferred_element_type=F32)
        o_blk = jnp.where(first_head, oe[:ATT_STEPS, :LANES], oe[ATT_STEPS:, :LANES])
        l_blk = jnp.where(first_head, oe[:ATT_STEPS, LANES:], oe[ATT_STEPS:, LANES:])
        m_blk = jnp.where(first_head, mb[:ATT_STEPS], mb[ATT_STEPS:])
        m_old = m_s[rows_q, :]
        m_new = jnp.maximum(m_old, m_blk)
        c_old = jnp.exp(m_old - m_new)
        c_blk = jnp.exp(m_blk - m_new)
        acc[rows_q, :] = acc[rows_q, :] * c_old + o_blk * c_blk
        l_s[rows_q, :] = l_s[rows_q, :] * c_old + l_blk * c_blk
        m_s[rows_q, :] = m_new

    n_sub = UNIT // ATT_STEPS
    for di, d in enumerate(DILATIONS):
        def body(t, carry, di=di, d=d):
            blk = t // d
            r = t - blk * d
            qs = blk * (ATT_STEPS * d) + r
            ks = UNIT + qs - ATT_STEPS * d
            substep(di, d, qs, ks, jnp.logical_and(u == 0, blk == 0))
            return carry
        lax.fori_loop(0, n_sub, body, 0)

    o_ref[...] = (acc[...] / l_s[...]).astype(BF16)


def _attention(proj3, slopes):
    bsz, seq, _ = proj3.shape
    blk = (None, UNIT, LANES)
    prev = lambda b, h, u: jnp.maximum(u - 1, 0)
    return pl.pallas_call(
        _attn_body,
        grid=(bsz, HEAD_PAIRS, seq // UNIT),
        in_specs=[
            pl.BlockSpec(memory_space=pltpu.SMEM),
            pl.BlockSpec(blk, lambda b, h, u: (b, u, h)),
            pl.BlockSpec(blk, lambda b, h, u: (b, u, HEAD_PAIRS + h)),
            pl.BlockSpec(blk, lambda b, h, u: (b, prev(b, h, u), HEAD_PAIRS + h)),
            pl.BlockSpec(blk, lambda b, h, u: (b, u, 2 * HEAD_PAIRS + h)),
            pl.BlockSpec(blk, lambda b, h, u: (b, prev(b, h, u), 2 * HEAD_PAIRS + h)),
        ],
        out_specs=pl.BlockSpec(blk, lambda b, h, u: (b, u, h)),
        out_shape=jax.ShapeDtypeStruct((bsz, seq, D_ATT), BF16),
        scratch_shapes=[
            pltpu.VMEM((UNIT, LANES), F32),
            pltpu.VMEM((UNIT, LANES), F32),
            pltpu.VMEM((2 * UNIT, LANES), F32),
            pltpu.VMEM((2 * UNIT, LANES), F32),
            pltpu.VMEM((UNIT, LANES), F32),
            pltpu.VMEM((UNIT, LANES), F32),
            pltpu.VMEM((UNIT, LANES), F32),
            pltpu.VMEM((2 * len(DILATIONS), 2 * ATT_STEPS, 2 * ATT_STEPS), F32),
        ],
        compiler_params=pltpu.CompilerParams(
            dimension_semantics=("parallel", "parallel", "arbitrary"), vmem_limit_bytes=VMEM_LIMIT),
        name="dilated_attention",
    )(slopes, proj3, proj3, proj3, proj3, proj3)


def _split_bf16(v, parts):
    out = []
    for _ in range(parts):
        hi = v.astype(BF16)
        out.append(hi)
        v = v - hi.astype(F32)
    return out


def _ssd_body(xs_ref, bc_ref, z_ref, dt_ref, cwx_ref, cbx_ref, cwb_ref, cbb_ref,
              dtb_ref, alog_ref, dskip_ref, g_ref, o_ref, xpad, bpad, state):
    halo = 8

    @pl.when(pl.program_id(1) == 0)
    def _():
        xpad[0:halo, :] = jnp.zeros((halo, D_SSM), F32)
        bpad[0:halo, :] = jnp.zeros((halo, D_BC), F32)
        state[...] = jnp.zeros_like(state)

    xpad[halo:halo + CHUNK, :] = xs_ref[...].astype(F32)
    bpad[halo:halo + CHUNK, :] = bc_ref[...].astype(F32)

    def conv(pad, w_ref, b_ref):
        out = b_ref[...]
        for j in range(CONV_WIDTH):
            o = halo - (CONV_WIDTH - 1) + j
            out = out + w_ref[j:j + 1, :] * pad[o:o + CHUNK, :]
        return _silu(out)

    xs = conv(xpad, cwx_ref, cbx_ref)
    bcv = conv(bpad, cwb_ref, cbb_ref)
    xpad[0:halo, :] = xpad[CHUNK:CHUNK + halo, :]
    bpad[0:halo, :] = bpad[CHUNK:CHUNK + halo, :]

    lane = lax.broadcasted_iota(jnp.int32, (1, LANES), 1)
    first_head = lane < HEAD_DIM
    dt_in = dt_ref[...] + dtb_ref[...]
    dt = jnp.maximum(dt_in, 0.0) + jnp.log1p(jnp.exp(-jnp.abs(dt_in)))
    a = jnp.where(lane < SSM_HEADS, -jnp.exp(alog_ref[...]), 0.0)
    adt = dt * a

    ri = lax.broadcasted_iota(jnp.int32, (CHUNK, CHUNK), 0)
    ci = lax.broadcasted_iota(jnp.int32, (CHUNK, CHUNK), 1)
    tril = ri >= ci
    tril_b = jnp.where(tril, 1.0, 0.0).astype(BF16)
    a_cs = sum(jnp.dot(tril_b, part, preferred_element_type=F32) for part in _split_bf16(adt, 3))
    a_last = a_cs[CHUNK - 1:CHUNK, :]
    a_cs_t = a_cs.T
    dt_t = dt.T
    w_t = (dt * jnp.exp(a_last - a_cs)).T

    y_pairs = []
    for g in range(SSM_GROUPS):
        bg = bcv[:, g * SSM_STATE:(g + 1) * SSM_STATE]
        cg = bcv[:, (SSM_GROUPS + g) * SSM_STATE:(SSM_GROUPS + g + 1) * SSM_STATE]
        cb = lax.dot_general(cg.astype(BF16), bg.astype(BF16), (((1,), (1,)), ((), ())),
                             preferred_element_type=F32)
        bg_t = bg.T
        for pair in range(g * HEAD_PAIRS // SSM_GROUPS, (g + 1) * HEAD_PAIRS // SSM_GROUPS):
            cols = slice(pair * LANES, (pair + 1) * LANES)
            xs_b = xs[:, cols].astype(BF16)
            prev = state[:, cols]
            rhs = jnp.concatenate([xs_b, prev.astype(BF16)], axis=0)
            ys, sts = [], []
            for h in (2 * pair, 2 * pair + 1):
                col_h = jnp.broadcast_to(a_cs[:, h:h + 1], (CHUNK, CHUNK))
                row_h = jnp.broadcast_to(a_cs_t[h:h + 1, :], (CHUNK, CHUNK))
                decay = jnp.exp(jnp.where(tril, col_h - row_h, NEG))
                m_h = cb * decay * dt_t[h:h + 1, :]
                e_h = cg * jnp.exp(col_h)
                lhs = jnp.concatenate([m_h, e_h], axis=1).astype(BF16)
                ys.append(jnp.dot(lhs, rhs, preferred_element_type=F32))
                w_h = (bg_t * w_t[h:h + 1, :]).astype(BF16)
                sts.append(jnp.dot(w_h, xs_b, preferred_element_type=F32))
            y_pairs.append(jnp.where(first_head, ys[0], ys[1]))
            chunk_decay = jnp.exp(jnp.where(first_head, a_last[:, 2 * pair:2 * pair + 1],
                                            a_last[:, 2 * pair + 1:2 * pair + 2]))
            state[:, cols] = prev * chunk_decay + jnp.where(first_head, sts[0], sts[1])

    y = jnp.concatenate(y_pairs, axis=1) + dskip_ref[...] * xs
    y = y * _silu(z_ref[...].astype(F32))
    y = y * lax.rsqrt(jnp.mean(y * y, axis=-1, keepdims=True) + NORM_EPS) * g_ref[...]
    o_ref[...] = y.astype(BF16)


def _ssd(proj3, dt3, cwx, cbx, cwb, cbb, dtb, alog, dskip, gain):
    bsz, seq, _ = proj3.shape
    full = lambda shape: pl.BlockSpec(shape, lambda b, c: (0,) * len(shape))
    return pl.pallas_call(
        _ssd_body,
        grid=(bsz, seq // CHUNK),
        in_specs=[
            pl.BlockSpec((None, CHUNK, D_SSM), lambda b, c: (b, c, COL_XS)),
            pl.BlockSpec((None, CHUNK, D_BC), lambda b, c: (b, c, (COL_XS + 1) * D_SSM // D_BC)),
            pl.BlockSpec((None, CHUNK, D_SSM), lambda b, c: (b, c, COL_Z_SSM)),
            pl.BlockSpec((None, CHUNK, LANES), lambda b, c: (b, c, 0)),
            full((CONV_WIDTH, D_SSM)), full((1, D_SSM)), full((CONV_WIDTH, D_BC)), full((1, D_BC)),
            full((1, LANES)), full((1, LANES)), full((1, D_SSM)), full((1, D_SSM)),
        ],
        out_specs=pl.BlockSpec((None, CHUNK, D_SSM), lambda b, c: (b, c, 0)),
        out_shape=jax.ShapeDtypeStruct((bsz, seq, D_SSM), BF16),
        scratch_shapes=[
            pltpu.VMEM((CHUNK + 8, D_SSM), F32),
            pltpu.VMEM((CHUNK + 8, D_BC), F32),
            pltpu.VMEM((SSM_STATE, D_SSM), F32),
        ],
        compiler_params=pltpu.CompilerParams(
            dimension_semantics=("parallel", "arbitrary"), vmem_limit_bytes=VMEM_LIMIT),
        name="ssd",
    )(proj3, proj3, proj3, dt3, cwx, cbx, cwb, cbb, dtb, alog, dskip, gain)


def _out_body(alpha, ya_ref, za_ref, ys_ref, x_ref, w_ref, ga_ref, lg_ref, lb_ref, o_ref):
    ya = ya_ref[...].astype(F32) * _silu(za_ref[...].astype(F32))
    ya = ya * lax.rsqrt(jnp.mean(ya * ya, axis=-1, keepdims=True) + NORM_EPS) * ga_ref[...]
    mix = jnp.concatenate([ya.astype(BF16), ys_ref[...]], axis=1)
    h = alpha * x_ref[...] + jnp.dot(mix, w_ref[...], preferred_element_type=F32)
    mu = jnp.mean(h, axis=-1, keepdims=True)
    hc = h - mu
    var = jnp.mean(hc * hc, axis=-1, keepdims=True)
    o_ref[...] = hc * lax.rsqrt(var + NORM_EPS) * lg_ref[...] + lb_ref[...]


def _out_proj(alpha, y_att, proj2, y_ssm, x2d, w_out, gain_att, ln_g, ln_b):
    m = x2d.shape[0]
    tm = 512
    rows = lambda shape: pl.BlockSpec(shape, lambda i: (i, 0))
    full = lambda shape: pl.BlockSpec(shape, lambda i: (0, 0))
    return pl.pallas_call(
        functools.partial(_out_body, alpha),
        grid=(m // tm,),
        in_specs=[
            rows((tm, D_ATT)),
            pl.BlockSpec((tm, D_ATT), lambda i: (i, COL_Z_ATT)),
            rows((tm, D_SSM)),
            rows((tm, D_MODEL)),
            full((D_ATT + D_SSM, D_MODEL)),
            full((1, D_ATT)), full((1, D_MODEL)), full((1, D_MODEL)),
        ],
        out_specs=rows((tm, D_MODEL)),
        out_shape=jax.ShapeDtypeStruct((m, D_MODEL), F32),
        compiler_params=pltpu.CompilerParams(
            dimension_semantics=("parallel",), vmem_limit_bytes=VMEM_LIMIT),
        name="out_proj",
    )(y_att, proj2, y_ssm, x2d, w_out, gain_att, ln_g, ln_b)


def _pad_lanes(v):
    return jnp.pad(v.astype(F32), (0, LANES - v.shape[0])).reshape(1, LANES)


def kernel(x, w_in, conv_w, conv_b, dt_bias, a_log, d_skip, att_norm_g, ssm_norm_g, w_out, ln_g, ln_b):
    bsz, seq, _ = x.shape
    depth = w_in.shape[0]
    assert seq % UNIT == 0
    alpha = (2.0 * depth) ** 0.25
    slopes = jnp.asarray(2.0 ** (-8.0 * np.arange(1, ATT_HEADS + 1) / ATT_HEADS), dtype=F32)
    for layer in range(depth):
        w_main = w_in[layer][:, :D_PROJ].astype(BF16)
        w_dt = jnp.pad(w_in[layer][:, D_PROJ:], ((0, 0), (0, LANES - SSM_HEADS))).astype(BF16)
        x2d = x.reshape(bsz * seq, D_MODEL)
        proj, dt_raw = _in_proj(x2d, w_main, w_dt)
        proj3 = proj.reshape(bsz, seq, D_PROJ)

        y_att = _attention(proj3, slopes)

        cw, cb = conv_w[layer].astype(F32), conv_b[layer].astype(F32)
        y_ssm = _ssd(
            proj3, dt_raw.reshape(bsz, seq, LANES),
            cw[:, :D_SSM], cb[:D_SSM].reshape(1, D_SSM), cw[:, D_SSM:], cb[D_SSM:].reshape(1, D_BC),
            _pad_lanes(dt_bias[layer]), _pad_lanes(a_log[layer]),
            jnp.repeat(d_skip[layer].astype(F32), HEAD_DIM).reshape(1, D_SSM),
            ssm_norm_g[layer].astype(F32).reshape(1, D_SSM))

        out = _out_proj(
            alpha, y_att.reshape(bsz * seq, D_ATT), proj, y_ssm.reshape(bsz * seq, D_SSM), x2d,
            w_out[layer].astype(BF16), att_norm_g[layer].astype(F32).reshape(1, D_ATT),
            ln_g[layer].astype(F32).reshape(1, D_MODEL), ln_b[layer].astype(F32).reshape(1, D_MODEL))
        x = out.reshape(bsz, seq, D_MODEL)
    return x
```

```python
import functools

import numpy as np
import jax
import jax.numpy as jnp
from jax import lax
from jax.experimental import pallas as pl
from jax.experimental.pallas import tpu as pltpu

F32 = jnp.float32
BF16 = jnp.bfloat16

D_MODEL = 1024
D_ATT = 1024
HEAD_DIM = 64
ATT_HEADS = D_ATT // HEAD_DIM
ATT_STEPS = 128
DILATIONS = (1, 4, 16)
D_SSM = 1024
SSM_HEADS = D_SSM // HEAD_DIM
SSM_GROUPS = 2
SSM_STATE = 128
CONV_WIDTH = 4
CHUNK = 128
D_BC = 2 * SSM_GROUPS * SSM_STATE
D_PROJ = 4 * D_ATT + D_SSM + D_SSM + D_BC
NORM_EPS = 1e-5

LANES = 128
HEAD_PAIRS = ATT_HEADS // 2
UNIT = DILATIONS[-1] * ATT_STEPS
BLOCKS_PER_ITER = 16
NEG = -1e30
VMEM_LIMIT = 48 * 1024 * 1024

COL_Z_ATT = 3
COL_Z_SSM = 4
COL_XS = 5


def _silu(v):
    return v / (1.0 + jnp.exp(-v))


def _lane_is_first_head():
    return lax.broadcasted_iota(jnp.int32, (1, LANES), 1) < HEAD_DIM


def _in_proj_body(x_ref, w_ref, wdt_ref, proj_ref, dt_ref, xb_ref):
    @pl.when(pl.program_id(1) == 0)
    def _():
        xb = x_ref[...].astype(BF16)
        xb_ref[...] = xb
        dt_ref[...] = jnp.dot(xb, wdt_ref[...], preferred_element_type=F32)

    proj_ref[...] = jnp.dot(xb_ref[...], w_ref[...], preferred_element_type=F32).astype(BF16)


def _in_proj(x2d, w_main, w_dt):
    m = x2d.shape[0]
    tm, tn = 1024, 512
    return pl.pallas_call(
        _in_proj_body,
        grid=(m // tm, D_PROJ // tn),
        in_specs=[
            pl.BlockSpec((tm, D_MODEL), lambda i, j: (i, 0)),
            pl.BlockSpec((D_MODEL, tn), lambda i, j: (0, j)),
            pl.BlockSpec((D_MODEL, LANES), lambda i, j: (0, 0)),
        ],
        out_specs=[
            pl.BlockSpec((tm, tn), lambda i, j: (i, j)),
            pl.BlockSpec((tm, LANES), lambda i, j: (i, 0)),
        ],
        out_shape=[
            jax.ShapeDtypeStruct((m, D_PROJ), BF16),
            jax.ShapeDtypeStruct((m, LANES), F32),
        ],
        scratch_shapes=[pltpu.VMEM((tm, D_MODEL), BF16)],
        compiler_params=pltpu.CompilerParams(
            dimension_semantics=("parallel", "arbitrary"), vmem_limit_bytes=VMEM_LIMIT),
        name="in_proj",
    )(x2d, w_main, w_dt)


def _attn_body(slope_ref, q_ref, kc_ref, kp_ref, vc_ref, vp_ref, o_ref,
               qf, kf, vf, qb, kb, vb, acc, m_s, l_s, bias):
    hp = pl.program_id(1)
    u = pl.program_id(2)
    first_head = _lane_is_first_head()

    q = q_ref[...] * (HEAD_DIM ** -0.5)
    qb[...] = q
    qf[...] = q.astype(F32)
    kf[0:UNIT, :] = kp_ref[...].astype(F32)
    kf[UNIT:2 * UNIT, :] = kc_ref[...].astype(F32)
    vf[0:UNIT, :] = vp_ref[...].astype(F32)
    vf[UNIT:2 * UNIT, :] = vc_ref[...].astype(F32)
    kb[0:ATT_STEPS, :] = kp_ref[UNIT - ATT_STEPS:UNIT, :]
    kb[ATT_STEPS:UNIT + ATT_STEPS, :] = kc_ref[...]
    vb[0:ATT_STEPS, :] = vp_ref[UNIT - ATT_STEPS:UNIT, :]
    vb[ATT_STEPS:UNIT + ATT_STEPS, :] = vc_ref[...]

    row = lax.broadcasted_iota(jnp.int32, (2 * ATT_STEPS, 2 * ATT_STEPS), 0)
    col = lax.broadcasted_iota(jnp.int32, (2 * ATT_STEPS, 2 * ATT_STEPS), 1)
    step = (row & (ATT_STEPS - 1)) + ATT_STEPS - col
    valid = (step >= 0) & (step <= ATT_STEPS)
    slope = jnp.where(row < ATT_STEPS, slope_ref[2 * hp], slope_ref[2 * hp + 1])
    for di, d in enumerate(DILATIONS):
        b = jnp.where(valid, -slope * (step * d).astype(F32), NEG)
        bias[2 * di] = b
        bias[2 * di + 1] = jnp.where(col < ATT_STEPS, NEG, b)

    def rows(d, start, n):
        return pl.ds(start, n) if d == 1 else pl.ds(start, n, stride=d)

    def stack_heads(q, zero):
        return jnp.concatenate([jnp.where(first_head, q, zero), jnp.where(first_head, zero, q)], axis=0)

    def load_block(d, qs):
        if d == 1:
            qs = pl.multiple_of(qs, ATT_STEPS)
            q = qb[pl.ds(qs, ATT_STEPS), :]
            window = pl.ds(qs, 2 * ATT_STEPS)
            return stack_heads(q, jnp.zeros_like(q)), kb[window, :], vb[window, :]
        window = rows(d, UNIT + qs - ATT_STEPS * d, 2 * ATT_STEPS)
        q2 = stack_heads(qf[rows(d, qs, ATT_STEPS), :], 0.0).astype(BF16)
        return q2, kf[window, :].astype(BF16), vf[window, :].astype(BF16)

    def block_softmax(di, d, qs, first):
        q2, kk, vv = load_block(d, qs)
        s = lax.dot_general(q2, kk, (((1,), (1,)), ((), ())), preferred_element_type=F32)
        s = s + bias[2 * di + first.astype(jnp.int32)]
        mb = jnp.max(s, axis=1, keepdims=True)
        p = jnp.exp(s - mb).astype(BF16)
        vext = jnp.concatenate([vv, jnp.ones_like(vv)], axis=1)
        oe = jnp.dot(p, vext, preferred_element_type=F32)
        o_blk = jnp.where(first_head, oe[:ATT_STEPS, :LANES], oe[ATT_STEPS:, :LANES])
        l_blk = jnp.where(first_head, oe[:ATT_STEPS, LANES:], oe[ATT_STEPS:, LANES:])
        m_blk = jnp.where(first_head, mb[:ATT_STEPS], mb[ATT_STEPS:])
        return o_blk, l_blk, m_blk

    def merge(d, qs, assign, o_blk, l_blk, m_blk):
        rows_q = rows(d, qs, ATT_STEPS)
        if assign:
            acc[rows_q, :] = o_blk
            l_s[rows_q, :] = l_blk
            m_s[rows_q, :] = m_blk
            return
        m_old = m_s[rows_q, :]
        m_new = jnp.maximum(m_old, m_blk)
        c_old = jnp.exp(m_old - m_new)
        c_blk = jnp.exp(m_blk - m_new)
        acc[rows_q, :] = acc[rows_q, :] * c_old + o_blk * c_blk
        l_s[rows_q, :] = l_s[rows_q, :] * c_old + l_blk * c_blk
        m_s[rows_q, :] = m_new

    n_sub = UNIT // ATT_STEPS
    order = tuple(reversed(range(len(DILATIONS))))
    for di in order:
        d = DILATIONS[di]

        def body(i, carry, di=di, d=d):
            starts, parts = [], []
            for g in range(BLOCKS_PER_ITER):
                t = i * BLOCKS_PER_ITER + g
                blk = t // d
                r = t - blk * d
                qs = blk * (ATT_STEPS * d) + r
                starts.append(qs)
                parts.append(block_softmax(di, d, qs, jnp.logical_and(u == 0, blk == 0)))
            for qs, part in zip(starts, parts):
                merge(d, qs, di == order[0], *part)
            return carry
        lax.fori_loop(0, n_sub // BLOCKS_PER_ITER, body, 0)

    o_ref[...] = (acc[...] / l_s[...]).astype(BF16)


def _attention(proj3, slopes):
    bsz, seq, _ = proj3.shape
    blk = (None, UNIT, LANES)
    prev = lambda b, h, u: jnp.maximum(u - 1, 0)
    return pl.pallas_call(
        _attn_body,
        grid=(bsz, HEAD_PAIRS, seq // UNIT),
        in_specs=[
            pl.BlockSpec(memory_space=pltpu.SMEM),
            pl.BlockSpec(blk, lambda b, h, u: (b, u, h)),
            pl.BlockSpec(blk, lambda b, h, u: (b, u, HEAD_PAIRS + h)),
            pl.BlockSpec(blk, lambda b, h, u: (b, prev(b, h, u), HEAD_PAIRS + h)),
            pl.BlockSpec(blk, lambda b, h, u: (b, u, 2 * HEAD_PAIRS + h)),
            pl.BlockSpec(blk, lambda b, h, u: (b, prev(b, h, u), 2 * HEAD_PAIRS + h)),
        ],
        out_specs=pl.BlockSpec(blk, lambda b, h, u: (b, u, h)),
        out_shape=jax.ShapeDtypeStruct((bsz, seq, D_ATT), BF16),
        scratch_shapes=[
            pltpu.VMEM((UNIT, LANES), F32),
            pltpu.VMEM((2 * UNIT, LANES), F32),
            pltpu.VMEM((2 * UNIT, LANES), F32),
            pltpu.VMEM((UNIT, LANES), BF16),
            pltpu.VMEM((UNIT + ATT_STEPS, LANES), BF16),
            pltpu.VMEM((UNIT + ATT_STEPS, LANES), BF16),
            pltpu.VMEM((UNIT, LANES), F32),
            pltpu.VMEM((UNIT, LANES), F32),
            pltpu.VMEM((UNIT, LANES), F32),
            pltpu.VMEM((2 * len(DILATIONS), 2 * ATT_STEPS, 2 * ATT_STEPS), F32),
        ],
        compiler_params=pltpu.CompilerParams(
            dimension_semantics=("parallel", "parallel", "arbitrary"), vmem_limit_bytes=VMEM_LIMIT),
        name="dilated_attention",
    )(slopes, proj3, proj3, proj3, proj3, proj3)


def _split_bf16(v, parts):
    out = []
    for _ in range(parts):
        hi = v.astype(BF16)
        out.append(hi)
        v = v - hi.astype(F32)
    return out


def _ssd_body(xs_ref, bc_ref, z_ref, dt_ref, cwx_ref, cbx_ref, cwb_ref, cbb_ref,
              dtb_ref, alog_ref, dskip_ref, g_ref, o_ref, xpad, bpad, state):
    halo = 8

    @pl.when(pl.program_id(1) == 0)
    def _():
        xpad[0:halo, :] = jnp.zeros((halo, D_SSM), F32)
        bpad[0:halo, :] = jnp.zeros((halo, D_BC), F32)
        state[...] = jnp.zeros_like(state)

    xpad[halo:halo + CHUNK, :] = xs_ref[...].astype(F32)
    bpad[halo:halo + CHUNK, :] = bc_ref[...].astype(F32)

    def conv(pad, w_ref, b_ref):
        out = b_ref[...]
        for j in range(CONV_WIDTH):
            o = halo - (CONV_WIDTH - 1) + j
            out = out + w_ref[j:j + 1, :] * pad[o:o + CHUNK, :]
        return _silu(out)

    xs = conv(xpad, cwx_ref, cbx_ref)
    bcv = conv(bpad, cwb_ref, cbb_ref)
    xpad[0:halo, :] = xpad[CHUNK:CHUNK + halo, :]
    bpad[0:halo, :] = bpad[CHUNK:CHUNK + halo, :]

    lane = lax.broadcasted_iota(jnp.int32, (1, LANES), 1)
    first_head = lane < HEAD_DIM
    dt_in = dt_ref[...] + dtb_ref[...]
    dt = jnp.maximum(dt_in, 0.0) + jnp.log1p(jnp.exp(-jnp.abs(dt_in)))
    a = jnp.where(lane < SSM_HEADS, -jnp.exp(alog_ref[...]), 0.0)
    adt = dt * a

    ri = lax.broadcasted_iota(jnp.int32, (CHUNK, CHUNK), 0)
    ci = lax.broadcasted_iota(jnp.int32, (CHUNK, CHUNK), 1)
    tril = ri >= ci
    tril_b = jnp.where(tril, 1.0, 0.0).astype(BF16)
    a_cs = sum(jnp.dot(tril_b, part, preferred_element_type=F32) for part in _split_bf16(adt, 3))
    a_last = a_cs[CHUNK - 1:CHUNK, :]
    a_cs_t = a_cs.T
    dt_t = dt.T
    w_t = (dt * jnp.exp(a_last - a_cs)).T

    y_pairs = []
    for g in range(SSM_GROUPS):
        bg = bcv[:, g * SSM_STATE:(g + 1) * SSM_STATE]
        cg = bcv[:, (SSM_GROUPS + g) * SSM_STATE:(SSM_GROUPS + g + 1) * SSM_STATE]
        cb = lax.dot_general(cg.astype(BF16), bg.astype(BF16), (((1,), (1,)), ((), ())),
                             preferred_element_type=F32)
        bg_t = bg.T
        for pair in range(g * HEAD_PAIRS // SSM_GROUPS, (g + 1) * HEAD_PAIRS // SSM_GROUPS):
            cols = slice(pair * LANES, (pair + 1) * LANES)
            xs_b = xs[:, cols].astype(BF16)
            prev = state[:, cols]
            rhs = jnp.concatenate([xs_b, prev.astype(BF16)], axis=0)
            ys, sts = [], []
            for h in (2 * pair, 2 * pair + 1):
                col_h = jnp.broadcast_to(a_cs[:, h:h + 1], (CHUNK, CHUNK))
                row_h = jnp.broadcast_to(a_cs_t[h:h + 1, :], (CHUNK, CHUNK))
                decay = jnp.exp(jnp.where(tril, col_h - row_h, NEG))
                m_h = cb * decay * dt_t[h:h + 1, :]
                e_h = cg * jnp.exp(col_h)
                lhs = jnp.concatenate([m_h, e_h], axis=1).astype(BF16)
                ys.append(jnp.dot(lhs, rhs, preferred_element_type=F32))
                w_h = (bg_t * w_t[h:h + 1, :]).astype(BF16)
                sts.append(jnp.dot(w_h, xs_b, preferred_element_type=F32))
            y_pairs.append(jnp.where(first_head, ys[0], ys[1]))
            chunk_decay = jnp.exp(jnp.where(first_head, a_last[:, 2 * pair:2 * pair + 1],
                                            a_last[:, 2 * pair + 1:2 * pair + 2]))
            state[:, cols] = prev * chunk_decay + jnp.where(first_head, sts[0], sts[1])

    y = jnp.concatenate(y_pairs, axis=1) + dskip_ref[...] * xs
    y = y * _silu(z_ref[...].astype(F32))
    y = y * lax.rsqrt(jnp.mean(y * y, axis=-1, keepdims=True) + NORM_EPS) * g_ref[...]
    o_ref[...] = y.astype(BF16)


def _ssd(proj3, dt3, cwx, cbx, cwb, cbb, dtb, alog, dskip, gain):
    bsz, seq, _ = proj3.shape
    full = lambda shape: pl.BlockSpec(shape, lambda b, c: (0,) * len(shape))
    return pl.pallas_call(
        _ssd_body,
        grid=(bsz, seq // CHUNK),
        in_specs=[
            pl.BlockSpec((None, CHUNK, D_SSM), lambda b, c: (b, c, COL_XS)),
            pl.BlockSpec((None, CHUNK, D_BC), lambda b, c: (b, c, (COL_XS + 1) * D_SSM // D_BC)),
            pl.BlockSpec((None, CHUNK, D_SSM), lambda b, c: (b, c, COL_Z_SSM)),
            pl.BlockSpec((None, CHUNK, LANES), lambda b, c: (b, c, 0)),
            full((CONV_WIDTH, D_SSM)), full((1, D_SSM)), full((CONV_WIDTH, D_BC)), full((1, D_BC)),
            full((1, LANES)), full((1, LANES)), full((1, D_SSM)), full((1, D_SSM)),
        ],
        out_specs=pl.BlockSpec((None, CHUNK, D_SSM), lambda b, c: (b, c, 0)),
        out_shape=jax.ShapeDtypeStruct((bsz, seq, D_SSM), BF16),
        scratch_shapes=[
            pltpu.VMEM((CHUNK + 8, D_SSM), F32),
            pltpu.VMEM((CHUNK + 8, D_BC), F32),
            pltpu.VMEM((SSM_STATE, D_SSM), F32),
        ],
        compiler_params=pltpu.CompilerParams(
            dimension_semantics=("parallel", "arbitrary"), vmem_limit_bytes=VMEM_LIMIT),
        name="ssd",
    )(proj3, proj3, proj3, dt3, cwx, cbx, cwb, cbb, dtb, alog, dskip, gain)


def _out_body(alpha, ya_ref, za_ref, ys_ref, x_ref, w_ref, ga_ref, lg_ref, lb_ref, o_ref):
    ya = ya_ref[...].astype(F32) * _silu(za_ref[...].astype(F32))
    ya = ya * lax.rsqrt(jnp.mean(ya * ya, axis=-1, keepdims=True) + NORM_EPS) * ga_ref[...]
    mix = jnp.concatenate([ya.astype(BF16), ys_ref[...]], axis=1)
    h = alpha * x_ref[...] + jnp.dot(mix, w_ref[...], preferred_element_type=F32)
    mu = jnp.mean(h, axis=-1, keepdims=True)
    hc = h - mu
    var = jnp.mean(hc * hc, axis=-1, keepdims=True)
    o_ref[...] = hc * lax.rsqrt(var + NORM_EPS) * lg_ref[...] + lb_ref[...]


def _out_proj(alpha, y_att, proj2, y_ssm, x2d, w_out, gain_att, ln_g, ln_b):
    m = x2d.shape[0]
    tm = 512
    rows = lambda shape: pl.BlockSpec(shape, lambda i: (i, 0))
    full = lambda shape: pl.BlockSpec(shape, lambda i: (0, 0))
    return pl.pallas_call(
        functools.partial(_out_body, alpha),
        grid=(m // tm,),
        in_specs=[
            rows((tm, D_ATT)),
            pl.BlockSpec((tm, D_ATT), lambda i: (i, COL_Z_ATT)),
            rows((tm, D_SSM)),
            rows((tm, D_MODEL)),
            full((D_ATT + D_SSM, D_MODEL)),
            full((1, D_ATT)), full((1, D_MODEL)), full((1, D_MODEL)),
        ],
        out_specs=rows((tm, D_MODEL)),
        out_shape=jax.ShapeDtypeStruct((m, D_MODEL), F32),
        compiler_params=pltpu.CompilerParams(
            dimension_semantics=("parallel",), vmem_limit_bytes=VMEM_LIMIT),
        name="out_proj",
    )(y_att, proj2, y_ssm, x2d, w_out, gain_att, ln_g, ln_b)


def _pad_lanes(v):
    return jnp.pad(v.astype(F32), (0, LANES - v.shape[0])).reshape(1, LANES)


def kernel(x, w_in, conv_w, conv_b, dt_bias, a_log, d_skip, att_norm_g, ssm_norm_g, w_out, ln_g, ln_b):
    bsz, seq, _ = x.shape
    depth = w_in.shape[0]
    assert seq % UNIT == 0
    alpha = (2.0 * depth) ** 0.25
    slopes = jnp.asarray(2.0 ** (-8.0 * np.arange(1, ATT_HEADS + 1) / ATT_HEADS), dtype=F32)
    for layer in range(depth):
        w_main = w_in[layer][:, :D_PROJ].astype(BF16)
        w_dt = jnp.pad(w_in[layer][:, D_PROJ:], ((0, 0), (0, LANES - SSM_HEADS))).astype(BF16)
        x2d = x.reshape(bsz * seq, D_MODEL)
        proj, dt_raw = _in_proj(x2d, w_main, w_dt)
        proj3 = proj.reshape(bsz, seq, D_PROJ)

        y_att = _attention(proj3, slopes)

        cw, cb = conv_w[layer].astype(F32), conv_b[layer].astype(F32)
        y_ssm = _ssd(
            proj3, dt_raw.reshape(bsz, seq, LANES),
            cw[:, :D_SSM], cb[:D_SSM].reshape(1, D_SSM), cw[:, D_SSM:], cb[D_SSM:].reshape(1, D_BC),
            _pad_lanes(dt_bias[layer]), _pad_lanes(a_log[layer]),
            jnp.repeat(d_skip[layer].astype(F32), HEAD_DIM).reshape(1, D_SSM),
            ssm_norm_g[layer].astype(F32).reshape(1, D_SSM))

        out = _out_proj(
            alpha, y_att.reshape(bsz * seq, D_ATT), proj, y_ssm.reshape(bsz * seq, D_SSM), x2d,
            w_out[layer].astype(BF16), att_norm_g[layer].astype(F32).reshape(1, D_ATT),
            ln_g[layer].astype(F32).reshape(1, D_MODEL), ln_b[layer].astype(F32).reshape(1, D_MODEL))
        x = out.reshape(bsz, seq, D_MODEL)
    return x
```

```python
import functools

import numpy as np
import jax
import jax.numpy as jnp
from jax import lax
from jax.experimental import pallas as pl
from jax.experimental.pallas import tpu as pltpu

F32 = jnp.float32
BF16 = jnp.bfloat16

D_MODEL = 1024
D_ATT = 1024
HEAD_DIM = 64
ATT_HEADS = D_ATT // HEAD_DIM
ATT_STEPS = 128
DILATIONS = (1, 4, 16)
D_SSM = 1024
SSM_HEADS = D_SSM // HEAD_DIM
SSM_GROUPS = 2
SSM_STATE = 128
CONV_WIDTH = 4
CHUNK = 128
D_BC = 2 * SSM_GROUPS * SSM_STATE
D_PROJ = 4 * D_ATT + D_SSM + D_SSM + D_BC
D_XBC = D_SSM + D_BC
D_XBC_START = D_PROJ - D_XBC
NORM_EPS = 1e-5

LANES = 128
HEAD_PAIRS = ATT_HEADS // 2
UNIT = DILATIONS[-1] * ATT_STEPS
BLOCKS_PER_ITER = 16
NEG = -1e30
PROJ_ROWS, PROJ_COLS = 2048, 512
CONV_TAIL = 16
DECAY_CHUNKS = 8
VMEM_LIMIT = 56 * 1024 * 1024

COL_Z_ATT = 3
COL_Z_SSM = 4
COL_XS = 5


def _silu(v):
    h = 0.5 * v
    return h + h * jnp.tanh(h)


def _lane_is_first_head():
    return lax.broadcasted_iota(jnp.int32, (1, LANES), 1) < HEAD_DIM


def _in_proj_body(x_ref, w_ref, wdt_ref, proj_ref, dt_ref, xb_ref):
    @pl.when(pl.program_id(1) == 0)
    def _():
        xb_ref[...] = x_ref[...].astype(BF16)
        dt_ref[...] = jnp.dot(xb_ref[...], wdt_ref[...], preferred_element_type=F32)

    def project():
        return jnp.dot(xb_ref[...], w_ref[...], preferred_element_type=F32)

    j = pl.program_id(1)
    is_gate = jnp.logical_and(j >= COL_Z_ATT * D_ATT // PROJ_COLS, j < COL_XS * D_SSM // PROJ_COLS)

    @pl.when(is_gate)
    def _():
        proj_ref[...] = _silu(project()).astype(BF16)

    @pl.when(jnp.logical_not(is_gate))
    def _():
        proj_ref[...] = project().astype(BF16)


def _in_proj(x2d, w_main, w_dt):
    m = x2d.shape[0]
    tm, tn = PROJ_ROWS, PROJ_COLS
    assert m % tm == 0
    return pl.pallas_call(
        _in_proj_body,
        grid=(m // tm, D_PROJ // tn),
        in_specs=[
            pl.BlockSpec((tm, D_MODEL), lambda i, j: (i, 0)),
            pl.BlockSpec((D_MODEL, tn), lambda i, j: (0, j)),
            pl.BlockSpec((D_MODEL, LANES), lambda i, j: (0, 0)),
        ],
        out_specs=[
            pl.BlockSpec((tm, tn), lambda i, j: (i, j)),
            pl.BlockSpec((tm, LANES), lambda i, j: (i, 0)),
        ],
        out_shape=[
            jax.ShapeDtypeStruct((m, D_PROJ), BF16),
            jax.ShapeDtypeStruct((m, LANES), F32),
        ],
        scratch_shapes=[pltpu.VMEM((tm, D_MODEL), BF16)],
        compiler_params=pltpu.CompilerParams(
            dimension_semantics=("parallel", "arbitrary"), vmem_limit_bytes=VMEM_LIMIT),
        name="in_proj",
    )(x2d, w_main, w_dt)


def _attn_body(slope_ref, q_ref, kc_ref, kp_ref, vc_ref, vp_ref, o_ref,
               qf, kf, vf, qb, kb, vb, acc, m_s, l_s, bias):
    hp = pl.program_id(1)
    u = pl.program_id(2)
    first_head = _lane_is_first_head()

    q = q_ref[...] * (HEAD_DIM ** -0.5)
    qb[...] = q
    qf[...] = q.astype(F32)
    kf[0:UNIT, :] = kp_ref[...].astype(F32)
    kf[UNIT:2 * UNIT, :] = kc_ref[...].astype(F32)
    vf[0:UNIT, :] = vp_ref[...].astype(F32)
    vf[UNIT:2 * UNIT, :] = vc_ref[...].astype(F32)
    kb[0:ATT_STEPS, :] = kp_ref[UNIT - ATT_STEPS:UNIT, :]
    kb[ATT_STEPS:UNIT + ATT_STEPS, :] = kc_ref[...]
    vb[0:ATT_STEPS, :] = vp_ref[UNIT - ATT_STEPS:UNIT, :]
    vb[ATT_STEPS:UNIT + ATT_STEPS, :] = vc_ref[...]

    row = lax.broadcasted_iota(jnp.int32, (2 * ATT_STEPS, 2 * ATT_STEPS), 0)
    col = lax.broadcasted_iota(jnp.int32, (2 * ATT_STEPS, 2 * ATT_STEPS), 1)
    step = (row & (ATT_STEPS - 1)) + ATT_STEPS - col
    valid = (step >= 0) & (step <= ATT_STEPS)
    slope = jnp.where(row < ATT_STEPS, slope_ref[2 * hp], slope_ref[2 * hp + 1])
    for di, d in enumerate(DILATIONS):
        b = jnp.where(valid, -slope * (step * d).astype(F32), NEG)
        bias[2 * di] = b
        bias[2 * di + 1] = jnp.where(col < ATT_STEPS, NEG, b)

    def rows(d, start, n):
        return pl.ds(start, n) if d == 1 else pl.ds(start, n, stride=d)

    def stack_heads(q, zero):
        return jnp.concatenate([jnp.where(first_head, q, zero), jnp.where(first_head, zero, q)], axis=0)

    def load_block(d, qs):
        if d == 1:
            qs = pl.multiple_of(qs, ATT_STEPS)
            q = qb[pl.ds(qs, ATT_STEPS), :]
            window = pl.ds(qs, 2 * ATT_STEPS)
            return stack_heads(q, jnp.zeros_like(q)), kb[window, :], vb[window, :]
        window = rows(d, UNIT + qs - ATT_STEPS * d, 2 * ATT_STEPS)
        q2 = stack_heads(qf[rows(d, qs, ATT_STEPS), :], 0.0).astype(BF16)
        return q2, kf[window, :].astype(BF16), vf[window, :].astype(BF16)

    def block_softmax(di, d, qs, first):
        q2, kk, vv = load_block(d, qs)
        s = lax.dot_general(q2, kk, (((1,), (1,)), ((), ())), preferred_element_type=F32)
        s = s + bias[2 * di + first.astype(jnp.int32)]
        mb = jnp.max(s, axis=1, keepdims=True)
        p = jnp.exp(s - mb).astype(BF16)
        vext = jnp.concatenate([vv, jnp.ones_like(vv)], axis=1)
        oe = jnp.dot(p, vext, preferred_element_type=F32)
        o_blk = jnp.where(first_head, oe[:ATT_STEPS, :LANES], oe[ATT_STEPS:, :LANES])
        l_blk = jnp.where(first_head, oe[:ATT_STEPS, LANES:], oe[ATT_STEPS:, LANES:])
        m_blk = jnp.where(first_head, mb[:ATT_STEPS], mb[ATT_STEPS:])
        return o_blk, l_blk, m_blk

    def merge(d, qs, assign, o_blk, l_blk, m_blk):
        rows_q = rows(d, qs, ATT_STEPS)
        if assign:
            acc[rows_q, :] = o_blk
            l_s[rows_q, :] = l_blk
            m_s[rows_q, :] = m_blk
            return
        m_old = m_s[rows_q, :]
        m_new = jnp.maximum(m_old, m_blk)
        c_old = jnp.exp(m_old - m_new)
        c_blk = jnp.exp(m_blk - m_new)
        acc[rows_q, :] = acc[rows_q, :] * c_old + o_blk * c_blk
        l_s[rows_q, :] = l_s[rows_q, :] * c_old + l_blk * c_blk
        m_s[rows_q, :] = m_new

    n_sub = UNIT // ATT_STEPS
    order = tuple(reversed(range(len(DILATIONS))))
    for di in order:
        d = DILATIONS[di]

        def body(i, carry, di=di, d=d):
            starts, parts = [], []
            for g in range(BLOCKS_PER_ITER):
                t = i * BLOCKS_PER_ITER + g
                blk = t // d
                r = t - blk * d
                qs = blk * (ATT_STEPS * d) + r
                starts.append(qs)
                parts.append(block_softmax(di, d, qs, jnp.logical_and(u == 0, blk == 0)))
            for qs, part in zip(starts, parts):
                merge(d, qs, di == order[0], *part)
            return carry
        lax.fori_loop(0, n_sub // BLOCKS_PER_ITER, body, 0)

    o_ref[...] = (acc[...] / l_s[...]).astype(BF16)


def _attention(proj3, slopes):
    bsz, seq, _ = proj3.shape
    blk = (None, UNIT, LANES)
    prev = lambda b, h, u: jnp.maximum(u - 1, 0)
    return pl.pallas_call(
        _attn_body,
        grid=(bsz, HEAD_PAIRS, seq // UNIT),
        in_specs=[
            pl.BlockSpec(memory_space=pltpu.SMEM),
            pl.BlockSpec(blk, lambda b, h, u: (b, u, h)),
            pl.BlockSpec(blk, lambda b, h, u: (b, u, HEAD_PAIRS + h)),
            pl.BlockSpec(blk, lambda b, h, u: (b, prev(b, h, u), HEAD_PAIRS + h)),
            pl.BlockSpec(blk, lambda b, h, u: (b, u, 2 * HEAD_PAIRS + h)),
            pl.BlockSpec(blk, lambda b, h, u: (b, prev(b, h, u), 2 * HEAD_PAIRS + h)),
        ],
        out_specs=pl.BlockSpec(blk, lambda b, h, u: (b, u, h)),
        out_shape=jax.ShapeDtypeStruct((bsz, seq, D_ATT), BF16),
        scratch_shapes=[
            pltpu.VMEM((UNIT, LANES), F32),
            pltpu.VMEM((2 * UNIT, LANES), F32),
            pltpu.VMEM((2 * UNIT, LANES), F32),
            pltpu.VMEM((UNIT, LANES), BF16),
            pltpu.VMEM((UNIT + ATT_STEPS, LANES), BF16),
            pltpu.VMEM((UNIT + ATT_STEPS, LANES), BF16),
            pltpu.VMEM((UNIT, LANES), F32),
            pltpu.VMEM((UNIT, LANES), F32),
            pltpu.VMEM((UNIT, LANES), F32),
            pltpu.VMEM((2 * len(DILATIONS), 2 * ATT_STEPS, 2 * ATT_STEPS), F32),
        ],
        compiler_params=pltpu.CompilerParams(
            dimension_semantics=("parallel", "parallel", "arbitrary"), vmem_limit_bytes=VMEM_LIMIT),
        name="dilated_attention",
    )(slopes, proj3, proj3, proj3, proj3, proj3)


def _split_bf16(v, parts):
    out = []
    for _ in range(parts):
        hi = v.astype(BF16)
        out.append(hi)
        v = v - hi.astype(F32)
    return out


def _ssd_decay_body(dt_ref, dtb_ref, alog_ref, src_t_ref, acs_ref, w_ref):
    ri = lax.broadcasted_iota(jnp.int32, (CHUNK, CHUNK), 0)
    ci = lax.broadcasted_iota(jnp.int32, (CHUNK, CHUNK), 1)
    triu_b = jnp.where(ri <= ci, 1.0, 0.0).astype(BF16)
    neg_a = -jnp.exp(alog_ref[...])
    pad_rows = jnp.zeros((CHUNK - SSM_HEADS, CHUNK), F32)
    for g in range(DECAY_CHUNKS):
        rows = slice(g * CHUNK, (g + 1) * CHUNK)
        dt_in = dt_ref[rows, :].T[0:SSM_HEADS, :] + dtb_ref[...]
        dt_t = jnp.maximum(dt_in, 0.0) + jnp.log1p(jnp.exp(-jnp.abs(dt_in)))
        a_cs_t = sum(jnp.dot(part, triu_b, preferred_element_type=F32) for part in _split_bf16(dt_t * neg_a, 3))
        w_t = dt_t * jnp.exp(a_cs_t[:, CHUNK - 1:CHUNK] - a_cs_t)
        src_t_ref[g] = a_cs_t - jnp.log(dt_t)
        acs_ref[rows, :] = jnp.concatenate([a_cs_t, pad_rows], axis=0).T
        w_ref[rows, :] = jnp.concatenate([w_t, pad_rows], axis=0).T


def _ssd_decay(dt3, dtb, alog):
    bsz, seq, _ = dt3.shape
    n_chunks = seq // CHUNK
    rows = DECAY_CHUNKS * CHUNK
    full = lambda shape: pl.BlockSpec(shape, lambda b, i: (0,) * len(shape))
    pos_major = pl.BlockSpec((None, rows, LANES), lambda b, i: (b, i, 0))
    return pl.pallas_call(
        _ssd_decay_body,
        grid=(bsz, n_chunks // DECAY_CHUNKS),
        in_specs=[pos_major, full((SSM_HEADS, 1)), full((SSM_HEADS, 1))],
        out_specs=[
            pl.BlockSpec((None, DECAY_CHUNKS, SSM_HEADS, CHUNK), lambda b, i: (b, i, 0, 0)),
            pos_major, pos_major,
        ],
        out_shape=[
            jax.ShapeDtypeStruct((bsz, n_chunks, SSM_HEADS, CHUNK), F32),
            jax.ShapeDtypeStruct((bsz, seq, LANES), F32),
            jax.ShapeDtypeStruct((bsz, seq, LANES), F32),
        ],
        compiler_params=pltpu.CompilerParams(
            dimension_semantics=("parallel", "parallel"), vmem_limit_bytes=VMEM_LIMIT),
        name="ssd_decay",
    )(dt3, dtb, alog)


def _ssd_body(xs_ref, bc_ref, src_t_ref, acs_ref, wp_ref, sel_ref, cw_ref, cb_ref, dskip_ref, o_ref,
              upad, tail, state):
    @pl.when(pl.program_id(1) == 0)
    def _():
        upad[0:CHUNK, :] = jnp.zeros((CHUNK, D_XBC), BF16)
        tail[...] = jnp.zeros_like(tail)
        state[...] = jnp.zeros_like(state)

    upad[CHUNK - CONV_TAIL:CHUNK, :] = tail[...]
    upad[CHUNK:2 * CHUNK, 0:D_SSM] = xs_ref[...]
    upad[CHUNK:2 * CHUNK, D_SSM:D_XBC] = bc_ref[...]
    shifted = jnp.dot(sel_ref[...], upad[...], preferred_element_type=F32)
    conv = cb_ref[...] + cw_ref[CONV_WIDTH - 1:CONV_WIDTH, :] * upad[CHUNK:2 * CHUNK, :].astype(F32)
    for tap in range(CONV_WIDTH - 1):
        conv = conv + cw_ref[tap:tap + 1, :] * shifted[tap * CHUNK:(tap + 1) * CHUNK, :]
    xbc = _silu(conv)
    tail[:, 0:D_SSM] = xs_ref[CHUNK - CONV_TAIL:CHUNK, :]
    tail[:, D_SSM:D_XBC] = bc_ref[CHUNK - CONV_TAIL:CHUNK, :]

    xs = xbc[:, 0:D_SSM]
    xs_b = xs.astype(BF16)
    first_head = _lane_is_first_head()
    a_cs = acs_ref[...]
    w_p = wp_ref[...]
    src_t = src_t_ref[...]
    ri = lax.broadcasted_iota(jnp.int32, (CHUNK, CHUNK), 0)
    ci = lax.broadcasted_iota(jnp.int32, (CHUNK, CHUNK), 1)
    tril = ri >= ci

    def spread(v, pairs):
        return jnp.concatenate(
            [jnp.where(first_head, v[:, 2 * p:2 * p + 1], v[:, 2 * p + 1:2 * p + 2]) for p in pairs], axis=1)

    y_groups = []
    pairs_per_group = HEAD_PAIRS // SSM_GROUPS
    for g in range(SSM_GROUPS):
        pairs = range(g * pairs_per_group, (g + 1) * pairs_per_group)
        gcols = slice(pairs[0] * LANES, (pairs[-1] + 1) * LANES)
        bg = xbc[:, D_SSM + g * SSM_STATE:D_SSM + (g + 1) * SSM_STATE]
        cg = xbc[:, D_SSM + (SSM_GROUPS + g) * SSM_STATE:D_SSM + (SSM_GROUPS + g + 1) * SSM_STATE].astype(BF16)
        cb = lax.dot_general(cg, bg.astype(BF16), (((1,), (1,)), ((), ())), preferred_element_type=F32)
        y_diag = []
        for p in pairs:
            xs_p = xs_b[:, p * LANES:(p + 1) * LANES]
            zero = jnp.zeros_like(xs_p)
            m_pair = []
            for h in (2 * p, 2 * p + 1):
                seg = a_cs[:, h:h + 1] - src_t[h:h + 1, :]
                m_pair.append((cb * jnp.exp(jnp.where(tril, seg, NEG))).astype(BF16))
            rhs = jnp.concatenate([jnp.where(first_head, xs_p, zero), jnp.where(first_head, zero, xs_p)], axis=0)
            y_diag.append(jnp.dot(jnp.concatenate(m_pair, axis=1), rhs, preferred_element_type=F32))
        prev = state[:, gcols]
        y_off = jnp.dot(cg, prev.astype(BF16), preferred_element_type=F32)
        y_groups.append(jnp.concatenate(y_diag, axis=1) + y_off * jnp.exp(spread(a_cs, pairs)))
        xw = (xs[:, gcols] * spread(w_p, pairs)).astype(BF16)
        new = jnp.dot(bg.T.astype(BF16), xw, preferred_element_type=F32)
        state[:, gcols] = prev * jnp.exp(spread(a_cs[CHUNK - 1:CHUNK, :], pairs)) + new

    o_ref[...] = (jnp.concatenate(y_groups, axis=1) + dskip_ref[...] * xs).astype(BF16)


def _ssd(proj3, src_t, a_cs, w_p, conv_w, conv_b, dskip):
    bsz, seq, _ = proj3.shape
    sel = np.zeros(((CONV_WIDTH - 1) * CHUNK, 2 * CHUNK), np.float32)
    for tap in range(CONV_WIDTH - 1):
        sel[tap * CHUNK + np.arange(CHUNK), CHUNK - (CONV_WIDTH - 1) + tap + np.arange(CHUNK)] = 1.0
    full = lambda shape: pl.BlockSpec(shape, lambda b, c: (0,) * len(shape))
    pos_major = pl.BlockSpec((None, CHUNK, LANES), lambda b, c: (b, c, 0))
    return pl.pallas_call(
        _ssd_body,
        grid=(bsz, seq // CHUNK),
        in_specs=[
            pl.BlockSpec((None, CHUNK, D_SSM), lambda b, c: (b, c, COL_XS)),
            pl.BlockSpec((None, CHUNK, D_BC), lambda b, c: (b, c, (COL_XS + 1) * D_SSM // D_BC)),
            pl.BlockSpec((None, None, SSM_HEADS, CHUNK), lambda b, c: (b, c, 0, 0)),
            pos_major, pos_major,
            full(sel.shape), full((CONV_WIDTH, D_XBC)), full((1, D_XBC)), full((1, D_SSM)),
        ],
        out_specs=pl.BlockSpec((None, CHUNK, D_SSM), lambda b, c: (b, c, 0)),
        out_shape=jax.ShapeDtypeStruct((bsz, seq, D_SSM), BF16),
        scratch_shapes=[
            pltpu.VMEM((2 * CHUNK, D_XBC), BF16),
            pltpu.VMEM((CONV_TAIL, D_XBC), BF16),
            pltpu.VMEM((SSM_STATE, D_SSM), F32),
        ],
        compiler_params=pltpu.CompilerParams(
            dimension_semantics=("parallel", "arbitrary"), vmem_limit_bytes=VMEM_LIMIT),
        name="ssd",
    )(proj3, proj3, src_t, a_cs, w_p, jnp.asarray(sel, BF16), conv_w, conv_b, dskip)


def _out_body(alpha, ya_ref, za_ref, ys_ref, zs_ref, x_ref, w_ref, ga_ref, gs_ref, lg_ref, lb_ref, o_ref):
    def gated_rms_norm(y_ref, gate_ref, g_ref):
        y = y_ref[...].astype(F32) * gate_ref[...].astype(F32)
        return (y * lax.rsqrt(jnp.mean(y * y, axis=-1, keepdims=True) + NORM_EPS) * g_ref[...]).astype(BF16)

    mix = jnp.concatenate([gated_rms_norm(ya_ref, za_ref, ga_ref), gated_rms_norm(ys_ref, zs_ref, gs_ref)], axis=1)
    h = alpha * x_ref[...] + jnp.dot(mix, w_ref[...], preferred_element_type=F32)
    mu = jnp.mean(h, axis=-1, keepdims=True)
    hc = h - mu
    var = jnp.mean(hc * hc, axis=-1, keepdims=True)
    o_ref[...] = hc * lax.rsqrt(var + NORM_EPS) * lg_ref[...] + lb_ref[...]


def _out_proj(alpha, y_att, proj2, y_ssm, x2d, w_out, gain_att, gain_ssm, ln_g, ln_b):
    m = x2d.shape[0]
    tm = 512
    rows = lambda shape: pl.BlockSpec(shape, lambda i: (i, 0))
    full = lambda shape: pl.BlockSpec(shape, lambda i: (0, 0))
    return pl.pallas_call(
        functools.partial(_out_body, alpha),
        grid=(m // tm,),
        in_specs=[
            rows((tm, D_ATT)),
            pl.BlockSpec((tm, D_ATT), lambda i: (i, COL_Z_ATT)),
            rows((tm, D_SSM)),
            pl.BlockSpec((tm, D_SSM), lambda i: (i, COL_Z_SSM)),
            rows((tm, D_MODEL)),
            full((D_ATT + D_SSM, D_MODEL)),
            full((1, D_ATT)), full((1, D_SSM)), full((1, D_MODEL)), full((1, D_MODEL)),
        ],
        out_specs=rows((tm, D_MODEL)),
        out_shape=jax.ShapeDtypeStruct((m, D_MODEL), F32),
        compiler_params=pltpu.CompilerParams(
            dimension_semantics=("parallel",), vmem_limit_bytes=VMEM_LIMIT),
        name="out_proj",
    )(y_att, proj2, y_ssm, proj2, x2d, w_out, gain_att, gain_ssm, ln_g, ln_b)


def kernel(x, w_in, conv_w, conv_b, dt_bias, a_log, d_skip, att_norm_g, ssm_norm_g, w_out, ln_g, ln_b):
    bsz, seq, _ = x.shape
    depth = w_in.shape[0]
    assert seq % UNIT == 0
    alpha = (2.0 * depth) ** 0.25
    slopes = jnp.asarray(2.0 ** (-8.0 * np.arange(1, ATT_HEADS + 1) / ATT_HEADS), dtype=F32)
    for layer in range(depth):
        w_main = w_in[layer][:, :D_PROJ].astype(BF16)
        w_dt = jnp.pad(w_in[layer][:, D_PROJ:], ((0, 0), (0, LANES - SSM_HEADS))).astype(BF16)
        x2d = x.reshape(bsz * seq, D_MODEL)
        proj, dt_raw = _in_proj(x2d, w_main, w_dt)
        proj3 = proj.reshape(bsz, seq, D_PROJ)

        y_att = _attention(proj3, slopes)

        src_t, a_cs, w_p = _ssd_decay(
            dt_raw.reshape(bsz, seq, LANES),
            dt_bias[layer].astype(F32).reshape(SSM_HEADS, 1), a_log[layer].astype(F32).reshape(SSM_HEADS, 1))
        y_ssm = _ssd(
            proj3, src_t, a_cs, w_p, conv_w[layer].astype(F32), conv_b[layer].astype(F32).reshape(1, D_XBC),
            jnp.repeat(d_skip[layer].astype(F32), HEAD_DIM).reshape(1, D_SSM))

        out = _out_proj(
            alpha, y_att.reshape(bsz * seq, D_ATT), proj, y_ssm.reshape(bsz * seq, D_SSM), x2d,
            w_out[layer].astype(BF16), att_norm_g[layer].astype(F32).reshape(1, D_ATT),
            ssm_norm_g[layer].astype(F32).reshape(1, D_SSM),
            ln_g[layer].astype(F32).reshape(1, D_MODEL), ln_b[layer].astype(F32).reshape(1, D_MODEL))
        x = out.reshape(bsz, seq, D_MODEL)
    return x
```

```python
import functools

import numpy as np
import jax
import jax.numpy as jnp
from jax import lax
from jax.experimental import pallas as pl
from jax.experimental.pallas import tpu as pltpu

F32 = jnp.float32
BF16 = jnp.bfloat16

D_MODEL = 1024
D_ATT = 1024
HEAD_DIM = 64
ATT_HEADS = D_ATT // HEAD_DIM
ATT_STEPS = 128
DILATIONS = (1, 4, 16)
D_SSM = 1024
SSM_HEADS = D_SSM // HEAD_DIM
SSM_GROUPS = 2
SSM_STATE = 128
CONV_WIDTH = 4
CHUNK = 128
D_BC = 2 * SSM_GROUPS * SSM_STATE
D_PROJ = 4 * D_ATT + D_SSM + D_SSM + D_BC
D_XBC = D_SSM + D_BC
D_XBC_START = D_PROJ - D_XBC
NORM_EPS = 1e-5

LANES = 128
HEAD_PAIRS = ATT_HEADS // 2
UNIT = DILATIONS[-1] * ATT_STEPS
MIN_DENOMINATOR = 2.0 ** -100
LOG2_E = 1.4426950408889634
Q_SCALE = HEAD_DIM ** -0.5 * LOG2_E
NEG = -1e30
PROJ_ROWS, PROJ_COLS = 2048, 512
CONV_TAIL = 16
DECAY_CHUNKS = 8
VMEM_LIMIT = 56 * 1024 * 1024

COL_Z_ATT = 3
COL_Z_SSM = 4
COL_XS = 5


def _silu(v):
    h = 0.5 * v
    return h + h * jnp.tanh(h)


def _lane_is_first_head():
    return lax.broadcasted_iota(jnp.int32, (1, LANES), 1) < HEAD_DIM


def _in_proj_body(x_ref, w_ref, wdt_ref, proj_ref, dt_ref, xb_ref):
    @pl.when(pl.program_id(1) == 0)
    def _():
        xb_ref[...] = x_ref[...].astype(BF16)
        dt_ref[...] = jnp.dot(xb_ref[...], wdt_ref[...], preferred_element_type=F32)

    def project():
        return jnp.dot(xb_ref[...], w_ref[...], preferred_element_type=F32)

    j = pl.program_id(1)
    is_q = j < D_ATT // PROJ_COLS
    is_gate = jnp.logical_and(j >= COL_Z_ATT * D_ATT // PROJ_COLS, j < COL_XS * D_SSM // PROJ_COLS)

    @pl.when(is_q)
    def _():
        proj_ref[...] = (project() * Q_SCALE).astype(BF16)

    @pl.when(is_gate)
    def _():
        proj_ref[...] = _silu(project()).astype(BF16)

    @pl.when(jnp.logical_not(jnp.logical_or(is_q, is_gate)))
    def _():
        proj_ref[...] = project().astype(BF16)


def _in_proj(x2d, w_main, w_dt):
    m = x2d.shape[0]
    tm, tn = PROJ_ROWS, PROJ_COLS
    assert m % tm == 0
    return pl.pallas_call(
        _in_proj_body,
        grid=(m // tm, D_PROJ // tn),
        in_specs=[
            pl.BlockSpec((tm, D_MODEL), lambda i, j: (i, 0)),
            pl.BlockSpec((D_MODEL, tn), lambda i, j: (0, j)),
            pl.BlockSpec((D_MODEL, LANES), lambda i, j: (0, 0)),
        ],
        out_specs=[
            pl.BlockSpec((tm, tn), lambda i, j: (i, j)),
            pl.BlockSpec((tm, LANES), lambda i, j: (i, 0)),
        ],
        out_shape=[
            jax.ShapeDtypeStruct((m, D_PROJ), BF16),
            jax.ShapeDtypeStruct((m, LANES), F32),
        ],
        scratch_shapes=[pltpu.VMEM((tm, D_MODEL), BF16)],
        compiler_params=pltpu.CompilerParams(
            dimension_semantics=("parallel", "arbitrary"), vmem_limit_bytes=VMEM_LIMIT),
        name="in_proj",
    )(x2d, w_main, w_dt)


def _attn_body(slope_ref, q_ref, kc_ref, kp_ref, vc_ref, vp_ref, o_ref,
               qf, kf, vf, qb, kb, vb, acc, m_s, l_s, bias):
    hp = pl.program_id(1)
    u = pl.program_id(2)
    first_head = _lane_is_first_head()

    q = q_ref[...]
    qb[...] = q
    qf[...] = q.astype(F32)
    kf[0:UNIT, :] = kp_ref[...].astype(F32)
    kf[UNIT:2 * UNIT, :] = kc_ref[...].astype(F32)
    vf[0:UNIT, :] = vp_ref[...].astype(F32)
    vf[UNIT:2 * UNIT, :] = vc_ref[...].astype(F32)
    kb[0:ATT_STEPS, :] = kp_ref[UNIT - ATT_STEPS:UNIT, :]
    kb[ATT_STEPS:UNIT + ATT_STEPS, :] = kc_ref[...]
    vb[0:ATT_STEPS, :] = vp_ref[UNIT - ATT_STEPS:UNIT, :]
    vb[ATT_STEPS:UNIT + ATT_STEPS, :] = vc_ref[...]

    row = lax.broadcasted_iota(jnp.int32, (2 * ATT_STEPS, 2 * ATT_STEPS), 0)
    col = lax.broadcasted_iota(jnp.int32, (2 * ATT_STEPS, 2 * ATT_STEPS), 1)
    step = (row & (ATT_STEPS - 1)) + ATT_STEPS - col
    valid = (step >= 0) & (step <= ATT_STEPS)
    slope = jnp.where(row < ATT_STEPS, slope_ref[2 * hp], slope_ref[2 * hp + 1])
    for di, d in enumerate(DILATIONS):
        b = jnp.where(valid, -(slope * LOG2_E) * (step * d).astype(F32), NEG)
        bias[2 * di] = b
        bias[2 * di + 1] = jnp.where(col < ATT_STEPS, NEG, b)

    def rows(d, start, n):
        return pl.ds(start, n) if d == 1 else pl.ds(start, n, stride=d)

    def stack_heads(q, zero):
        return jnp.concatenate([jnp.where(first_head, q, zero), jnp.where(first_head, zero, q)], axis=0)

    def load_block(d, qs):
        if d == 1:
            qs = pl.multiple_of(qs, ATT_STEPS)
            q = qb[pl.ds(qs, ATT_STEPS), :]
            window = pl.ds(qs, 2 * ATT_STEPS)
            return stack_heads(q, jnp.zeros_like(q)), kb[window, :], vb[window, :]
        window = rows(d, UNIT + qs - ATT_STEPS * d, 2 * ATT_STEPS)
        q2 = stack_heads(qf[rows(d, qs, ATT_STEPS), :], 0.0).astype(BF16)
        return q2, kf[window, :].astype(BF16), vf[window, :].astype(BF16)

    def block_scores(di, d, qs, first):
        q2, kk, vv = load_block(d, qs)
        s = lax.dot_general(q2, kk, (((1,), (1,)), ((), ())), preferred_element_type=F32)
        return s + bias[2 * di + first.astype(jnp.int32)], vv

    def weighted_values(p, vv):
        vext = jnp.concatenate([vv, jnp.ones_like(vv)], axis=1)
        oe = jnp.dot(p, vext, preferred_element_type=F32)
        o_blk = jnp.where(first_head, oe[:ATT_STEPS, :LANES], oe[ATT_STEPS:, :LANES])
        l_blk = jnp.where(first_head, oe[:ATT_STEPS, LANES:], oe[ATT_STEPS:, LANES:])
        return o_blk, l_blk

    def block_start(d, t):
        blk = t // d
        r = t - blk * d
        return blk * (ATT_STEPS * d) + r, jnp.logical_and(u == 0, blk == 0)

    n_sub = UNIT // ATT_STEPS
    order = tuple(reversed(range(len(DILATIONS))))

    for di in order:
        d = DILATIONS[di]
        for t in range(n_sub):
            qs, first = block_start(d, t)
            s, vv = block_scores(di, d, qs, first)
            o_blk, l_blk = weighted_values(jnp.exp2(s).astype(BF16), vv)
            rows_q = rows(d, qs, ATT_STEPS)
            if di == order[0]:
                acc[rows_q, :] = o_blk
                l_s[rows_q, :] = l_blk
            else:
                acc[rows_q, :] += o_blk
                l_s[rows_q, :] += l_blk

    den = l_s[...]
    in_range = (den >= MIN_DENOMINATOR) & (den < jnp.inf) & (jnp.abs(acc[...]) < jnp.inf)
    n_bad = jnp.max(jnp.max(jnp.where(in_range, 0.0, 1.0), axis=1, keepdims=True), axis=0, keepdims=True)

    @pl.when(n_bad[0, 0] > 0.0)
    def _():
        for di in order:
            d = DILATIONS[di]

            def body(t, carry, di=di, d=d):
                qs, first = block_start(d, t)
                s, vv = block_scores(di, d, qs, first)
                mb = jnp.max(s, axis=1, keepdims=True)
                o_blk, l_blk = weighted_values(jnp.exp2(s - mb).astype(BF16), vv)
                m_blk = jnp.where(first_head, mb[:ATT_STEPS], mb[ATT_STEPS:])
                rows_q = rows(d, qs, ATT_STEPS)
                if di == order[0]:
                    acc[rows_q, :] = o_blk
                    l_s[rows_q, :] = l_blk
                    m_s[rows_q, :] = m_blk
                else:
                    m_old = m_s[rows_q, :]
                    m_new = jnp.maximum(m_old, m_blk)
                    c_old = jnp.exp2(m_old - m_new)
                    c_blk = jnp.exp2(m_blk - m_new)
                    acc[rows_q, :] = acc[rows_q, :] * c_old + o_blk * c_blk
                    l_s[rows_q, :] = l_s[rows_q, :] * c_old + l_blk * c_blk
                    m_s[rows_q, :] = m_new
                return carry
            lax.fori_loop(0, n_sub, body, 0)

    o_ref[...] = (acc[...] / l_s[...]).astype(BF16)


def _attention(proj3, slopes):
    bsz, seq, _ = proj3.shape
    blk = (None, UNIT, LANES)
    prev = lambda b, h, u: jnp.maximum(u - 1, 0)
    return pl.pallas_call(
        _attn_body,
        grid=(bsz, HEAD_PAIRS, seq // UNIT),
        in_specs=[
            pl.BlockSpec(memory_space=pltpu.SMEM),
            pl.BlockSpec(blk, lambda b, h, u: (b, u, h)),
            pl.BlockSpec(blk, lambda b, h, u: (b, u, HEAD_PAIRS + h)),
            pl.BlockSpec(blk, lambda b, h, u: (b, prev(b, h, u), HEAD_PAIRS + h)),
            pl.BlockSpec(blk, lambda b, h, u: (b, u, 2 * HEAD_PAIRS + h)),
            pl.BlockSpec(blk, lambda b, h, u: (b, prev(b, h, u), 2 * HEAD_PAIRS + h)),
        ],
        out_specs=pl.BlockSpec(blk, lambda b, h, u: (b, u, h)),
        out_shape=jax.ShapeDtypeStruct((bsz, seq, D_ATT), BF16),
        scratch_shapes=[
            pltpu.VMEM((UNIT, LANES), F32),
            pltpu.VMEM((2 * UNIT, LANES), F32),
            pltpu.VMEM((2 * UNIT, LANES), F32),
            pltpu.VMEM((UNIT, LANES), BF16),
            pltpu.VMEM((UNIT + ATT_STEPS, LANES), BF16),
            pltpu.VMEM((UNIT + ATT_STEPS, LANES), BF16),
            pltpu.VMEM((UNIT, LANES), F32),
            pltpu.VMEM((UNIT, LANES), F32),
            pltpu.VMEM((UNIT, LANES), F32),
            pltpu.VMEM((2 * len(DILATIONS), 2 * ATT_STEPS, 2 * ATT_STEPS), F32),
        ],
        compiler_params=pltpu.CompilerParams(
            dimension_semantics=("parallel", "parallel", "arbitrary"), vmem_limit_bytes=VMEM_LIMIT),
        name="dilated_attention",
    )(slopes, proj3, proj3, proj3, proj3, proj3)


def _split_bf16(v, parts):
    out = []
    for _ in range(parts):
        hi = v.astype(BF16)
        out.append(hi)
        v = v - hi.astype(F32)
    return out


def _ssd_decay_body(dt_ref, dtb_ref, alog_ref, src_t_ref, acs_ref, w_ref):
    ri = lax.broadcasted_iota(jnp.int32, (CHUNK, CHUNK), 0)
    ci = lax.broadcasted_iota(jnp.int32, (CHUNK, CHUNK), 1)
    triu_b = jnp.where(ri <= ci, 1.0, 0.0).astype(BF16)
    neg_a = -jnp.exp(alog_ref[...])
    pad_rows = jnp.zeros((CHUNK - SSM_HEADS, CHUNK), F32)
    for g in range(DECAY_CHUNKS):
        rows = slice(g * CHUNK, (g + 1) * CHUNK)
        dt_in = dt_ref[rows, :].T[0:SSM_HEADS, :] + dtb_ref[...]
        dt_t = jnp.maximum(dt_in, 0.0) + jnp.log1p(jnp.exp(-jnp.abs(dt_in)))
        a_cs_t = sum(jnp.dot(part, triu_b, preferred_element_type=F32) for part in _split_bf16(dt_t * neg_a, 3))
        w_t = dt_t * jnp.exp(a_cs_t[:, CHUNK - 1:CHUNK] - a_cs_t)
        src_t_ref[g] = a_cs_t - jnp.log(dt_t)
        acs_ref[rows, :] = jnp.concatenate([a_cs_t, pad_rows], axis=0).T
        w_ref[rows, :] = jnp.concatenate([w_t, pad_rows], axis=0).T


def _ssd_decay(dt3, dtb, alog):
    bsz, seq, _ = dt3.shape
    n_chunks = seq // CHUNK
    rows = DECAY_CHUNKS * CHUNK
    full = lambda shape: pl.BlockSpec(shape, lambda b, i: (0,) * len(shape))
    pos_major = pl.BlockSpec((None, rows, LANES), lambda b, i: (b, i, 0))
    return pl.pallas_call(
        _ssd_decay_body,
        grid=(bsz, n_chunks // DECAY_CHUNKS),
        in_specs=[pos_major, full((SSM_HEADS, 1)), full((SSM_HEADS, 1))],
        out_specs=[
            pl.BlockSpec((None, DECAY_CHUNKS, SSM_HEADS, CHUNK), lambda b, i: (b, i, 0, 0)),
            pos_major, pos_major,
        ],
        out_shape=[
            jax.ShapeDtypeStruct((bsz, n_chunks, SSM_HEADS, CHUNK), F32),
            jax.ShapeDtypeStruct((bsz, seq, LANES), F32),
            jax.ShapeDtypeStruct((bsz, seq, LANES), F32),
        ],
        compiler_params=pltpu.CompilerParams(
            dimension_semantics=("parallel", "parallel"), vmem_limit_bytes=VMEM_LIMIT),
        name="ssd_decay",
    )(dt3, dtb, alog)


def _ssd_body(xs_ref, bc_ref, src_t_ref, acs_ref, wp_ref, sel_ref, cw_ref, cb_ref, dskip_ref, o_ref,
              upad, tail, state):
    @pl.when(pl.program_id(1) == 0)
    def _():
        upad[0:CHUNK, :] = jnp.zeros((CHUNK, D_XBC), BF16)
        tail[...] = jnp.zeros_like(tail)
        state[...] = jnp.zeros_like(state)

    upad[CHUNK - CONV_TAIL:CHUNK, :] = tail[...]
    upad[CHUNK:2 * CHUNK, 0:D_SSM] = xs_ref[...]
    upad[CHUNK:2 * CHUNK, D_SSM:D_XBC] = bc_ref[...]
    shifted = jnp.dot(sel_ref[...], upad[...], preferred_element_type=F32)
    conv = cb_ref[...] + cw_ref[CONV_WIDTH - 1:CONV_WIDTH, :] * upad[CHUNK:2 * CHUNK, :].astype(F32)
    for tap in range(CONV_WIDTH - 1):
        conv = conv + cw_ref[tap:tap + 1, :] * shifted[tap * CHUNK:(tap + 1) * CHUNK, :]
    xbc = _silu(conv)
    tail[:, 0:D_SSM] = xs_ref[CHUNK - CONV_TAIL:CHUNK, :]
    tail[:, D_SSM:D_XBC] = bc_ref[CHUNK - CONV_TAIL:CHUNK, :]

    xs = xbc[:, 0:D_SSM]
    xs_b = xs.astype(BF16)
    first_head = _lane_is_first_head()
    a_cs = acs_ref[...]
    w_p = wp_ref[...]
    src_t = src_t_ref[...]
    ri = lax.broadcasted_iota(jnp.int32, (CHUNK, CHUNK), 0)
    ci = lax.broadcasted_iota(jnp.int32, (CHUNK, CHUNK), 1)
    tril = ri >= ci

    def spread(v, pairs):
        return jnp.concatenate(
            [jnp.where(first_head, v[:, 2 * p:2 * p + 1], v[:, 2 * p + 1:2 * p + 2]) for p in pairs], axis=1)

    y_groups = []
    pairs_per_group = HEAD_PAIRS // SSM_GROUPS
    for g in range(SSM_GROUPS):
        pairs = range(g * pairs_per_group, (g + 1) * pairs_per_group)
        gcols = slice(pairs[0] * LANES, (pairs[-1] + 1) * LANES)
        bg = xbc[:, D_SSM + g * SSM_STATE:D_SSM + (g + 1) * SSM_STATE]
        cg = xbc[:, D_SSM + (SSM_GROUPS + g) * SSM_STATE:D_SSM + (SSM_GROUPS + g + 1) * SSM_STATE].astype(BF16)
        cb = lax.dot_general(cg, bg.astype(BF16), (((1,), (1,)), ((), ())), preferred_element_type=F32)
        y_diag = []
        for p in pairs:
            xs_p = xs_b[:, p * LANES:(p + 1) * LANES]
            zero = jnp.zeros_like(xs_p)
            m_pair = []
            for h in (2 * p, 2 * p + 1):
                seg = a_cs[:, h:h + 1] - src_t[h:h + 1, :]
                m_pair.append((cb * jnp.exp(jnp.where(tril, seg, NEG))).astype(BF16))
            rhs = jnp.concatenate([jnp.where(first_head, xs_p, zero), jnp.where(first_head, zero, xs_p)], axis=0)
            y_diag.append(jnp.dot(jnp.concatenate(m_pair, axis=1), rhs, preferred_element_type=F32))
        prev = state[:, gcols]
        y_off = jnp.dot(cg, prev.astype(BF16), preferred_element_type=F32)
        y_groups.append(jnp.concatenate(y_diag, axis=1) + y_off * jnp.exp(spread(a_cs, pairs)))
        xw = (xs[:, gcols] * spread(w_p, pairs)).astype(BF16)
        new = jnp.dot(bg.T.astype(BF16), xw, preferred_element_type=F32)
        state[:, gcols] = prev * jnp.exp(spread(a_cs[CHUNK - 1:CHUNK, :], pairs)) + new

    o_ref[...] = (jnp.concatenate(y_groups, axis=1) + dskip_ref[...] * xs).astype(BF16)


def _ssd(proj3, src_t, a_cs, w_p, conv_w, conv_b, dskip):
    bsz, seq, _ = proj3.shape
    sel = np.zeros(((CONV_WIDTH - 1) * CHUNK, 2 * CHUNK), np.float32)
    for tap in range(CONV_WIDTH - 1):
        sel[tap * CHUNK + np.arange(CHUNK), CHUNK - (CONV_WIDTH - 1) + tap + np.arange(CHUNK)] = 1.0
    full = lambda shape: pl.BlockSpec(shape, lambda b, c: (0,) * len(shape))
    pos_major = pl.BlockSpec((None, CHUNK, LANES), lambda b, c: (b, c, 0))
    return pl.pallas_call(
        _ssd_body,
        grid=(bsz, seq // CHUNK),
        in_specs=[
            pl.BlockSpec((None, CHUNK, D_SSM), lambda b, c: (b, c, COL_XS)),
            pl.BlockSpec((None, CHUNK, D_BC), lambda b, c: (b, c, (COL_XS + 1) * D_SSM // D_BC)),
            pl.BlockSpec((None, None, SSM_HEADS, CHUNK), lambda b, c: (b, c, 0, 0)),
            pos_major, pos_major,
            full(sel.shape), full((CONV_WIDTH, D_XBC)), full((1, D_XBC)), full((1, D_SSM)),
        ],
        out_specs=pl.BlockSpec((None, CHUNK, D_SSM), lambda b, c: (b, c, 0)),
        out_shape=jax.ShapeDtypeStruct((bsz, seq, D_SSM), BF16),
        scratch_shapes=[
            pltpu.VMEM((2 * CHUNK, D_XBC), BF16),
            pltpu.VMEM((CONV_TAIL, D_XBC), BF16),
            pltpu.VMEM((SSM_STATE, D_SSM), F32),
        ],
        compiler_params=pltpu.CompilerParams(
            dimension_semantics=("parallel", "arbitrary"), vmem_limit_bytes=VMEM_LIMIT),
        name="ssd",
    )(proj3, proj3, src_t, a_cs, w_p, jnp.asarray(sel, BF16), conv_w, conv_b, dskip)


def _out_body(alpha, ya_ref, za_ref, ys_ref, zs_ref, x_ref, w_ref, ga_ref, gs_ref, lg_ref, lb_ref, o_ref):
    def gated_rms_norm(y_ref, gate_ref, g_ref):
        y = y_ref[...].astype(F32) * gate_ref[...].astype(F32)
        return (y * lax.rsqrt(jnp.mean(y * y, axis=-1, keepdims=True) + NORM_EPS) * g_ref[...]).astype(BF16)

    mix = jnp.concatenate([gated_rms_norm(ya_ref, za_ref, ga_ref), gated_rms_norm(ys_ref, zs_ref, gs_ref)], axis=1)
    h = alpha * x_ref[...] + jnp.dot(mix, w_ref[...], preferred_element_type=F32)
    mu = jnp.mean(h, axis=-1, keepdims=True)
    hc = h - mu
    var = jnp.mean(hc * hc, axis=-1, keepdims=True)
    o_ref[...] = hc * lax.rsqrt(var + NORM_EPS) * lg_ref[...] + lb_ref[...]


def _out_proj(alpha, y_att, proj2, y_ssm, x2d, w_out, gain_att, gain_ssm, ln_g, ln_b):
    m = x2d.shape[0]
    tm = 512
    rows = lambda shape: pl.BlockSpec(shape, lambda i: (i, 0))
    full = lambda shape: pl.BlockSpec(shape, lambda i: (0, 0))
    return pl.pallas_call(
        functools.partial(_out_body, alpha),
        grid=(m // tm,),
        in_specs=[
            rows((tm, D_ATT)),
            pl.BlockSpec((tm, D_ATT), lambda i: (i, COL_Z_ATT)),
            rows((tm, D_SSM)),
            pl.BlockSpec((tm, D_SSM), lambda i: (i, COL_Z_SSM)),
            rows((tm, D_MODEL)),
            full((D_ATT + D_SSM, D_MODEL)),
            full((1, D_ATT)), full((1, D_SSM)), full((1, D_MODEL)), full((1, D_MODEL)),
        ],
        out_specs=rows((tm, D_MODEL)),
        out_shape=jax.ShapeDtypeStruct((m, D_MODEL), F32),
        compiler_params=pltpu.CompilerParams(
            dimension_semantics=("parallel",), vmem_limit_bytes=VMEM_LIMIT),
        name="out_proj",
    )(y_att, proj2, y_ssm, proj2, x2d, w_out, gain_att, gain_ssm, ln_g, ln_b)


def kernel(x, w_in, conv_w, conv_b, dt_bias, a_log, d_skip, att_norm_g, ssm_norm_g, w_out, ln_g, ln_b):
    bsz, seq, _ = x.shape
    depth = w_in.shape[0]
    assert seq % UNIT == 0
    alpha = (2.0 * depth) ** 0.25
    slopes = jnp.asarray(2.0 ** (-8.0 * np.arange(1, ATT_HEADS + 1) / ATT_HEADS), dtype=F32)
    for layer in range(depth):
        w_main = w_in[layer][:, :D_PROJ].astype(BF16)
        w_dt = jnp.pad(w_in[layer][:, D_PROJ:], ((0, 0), (0, LANES - SSM_HEADS))).astype(BF16)
        x2d = x.reshape(bsz * seq, D_MODEL)
        proj, dt_raw = _in_proj(x2d, w_main, w_dt)
        proj3 = proj.reshape(bsz, seq, D_PROJ)

        y_att = _attention(proj3, slopes)

        src_t, a_cs, w_p = _ssd_decay(
            dt_raw.reshape(bsz, seq, LANES),
            dt_bias[layer].astype(F32).reshape(SSM_HEADS, 1), a_log[layer].astype(F32).reshape(SSM_HEADS, 1))
        y_ssm = _ssd(
            proj3, src_t, a_cs, w_p, conv_w[layer].astype(F32), conv_b[layer].astype(F32).reshape(1, D_XBC),
            jnp.repeat(d_skip[layer].astype(F32), HEAD_DIM).reshape(1, D_SSM))

        out = _out_proj(
            alpha, y_att.reshape(bsz * seq, D_ATT), proj, y_ssm.reshape(bsz * seq, D_SSM), x2d,
            w_out[layer].astype(BF16), att_norm_g[layer].astype(F32).reshape(1, D_ATT),
            ssm_norm_g[layer].astype(F32).reshape(1, D_SSM),
            ln_g[layer].astype(F32).reshape(1, D_MODEL), ln_b[layer].astype(F32).reshape(1, D_MODEL))
        x = out.reshape(bsz, seq, D_MODEL)
    return x
```

```python
import functools

import numpy as np
import jax
import jax.numpy as jnp
from jax import lax
from jax.experimental import pallas as pl
from jax.experimental.pallas import tpu as pltpu

F32 = jnp.float32
BF16 = jnp.bfloat16

D_MODEL = 1024
D_ATT = 1024
HEAD_DIM = 64
ATT_HEADS = D_ATT // HEAD_DIM
ATT_STEPS = 128
DILATIONS = (1, 4, 16)
D_SSM = 1024
SSM_HEADS = D_SSM // HEAD_DIM
SSM_GROUPS = 2
SSM_STATE = 128
CONV_WIDTH = 4
CHUNK = 128
D_BC = 2 * SSM_GROUPS * SSM_STATE
D_PROJ = 4 * D_ATT + D_SSM + D_SSM + D_BC
D_XBC = D_SSM + D_BC
D_XBC_START = D_PROJ - D_XBC
NORM_EPS = 1e-5

LANES = 128
HEAD_PAIRS = ATT_HEADS // 2
UNIT = DILATIONS[-1] * ATT_STEPS
MIN_DENOMINATOR = 2.0 ** -100
LOG2_E = 1.4426950408889634
Q_SCALE = HEAD_DIM ** -0.5 * LOG2_E
NEG = -1e30
PROJ_ROWS, PROJ_COLS = 2048, 512
CONV_TAIL = 16
OUT_ROWS = 512
DECAY_CHUNKS = 16
VMEM_LIMIT = 56 * 1024 * 1024

COL_Z_ATT = 3
COL_Z_SSM = 4
COL_XS = 5


def _silu(v):
    h = 0.5 * v
    return h + h * jnp.tanh(h)


def _lane_is_first_head():
    return lax.broadcasted_iota(jnp.int32, (1, LANES), 1) < HEAD_DIM


def _in_proj_body(x_ref, w_ref, wdt_ref, proj_ref, dt_ref, xb_ref):
    @pl.when(pl.program_id(1) == 0)
    def _():
        xb_ref[...] = x_ref[...].astype(BF16)
        dt_ref[...] = jnp.dot(xb_ref[...], wdt_ref[...], preferred_element_type=F32)

    def project():
        return jnp.dot(xb_ref[...], w_ref[...], preferred_element_type=F32)

    j = pl.program_id(1)
    is_q = j < D_ATT // PROJ_COLS
    is_gate = jnp.logical_and(j >= COL_Z_ATT * D_ATT // PROJ_COLS, j < COL_XS * D_SSM // PROJ_COLS)

    @pl.when(is_q)
    def _():
        proj_ref[...] = (project() * Q_SCALE).astype(BF16)

    @pl.when(is_gate)
    def _():
        proj_ref[...] = _silu(project()).astype(BF16)

    @pl.when(jnp.logical_not(jnp.logical_or(is_q, is_gate)))
    def _():
        proj_ref[...] = project().astype(BF16)


def _in_proj(x2d, w_main, w_dt):
    m = x2d.shape[0]
    tm, tn = PROJ_ROWS, PROJ_COLS
    assert m % tm == 0
    return pl.pallas_call(
        _in_proj_body,
        grid=(m // tm, D_PROJ // tn),
        in_specs=[
            pl.BlockSpec((tm, D_MODEL), lambda i, j: (i, 0)),
            pl.BlockSpec((D_MODEL, tn), lambda i, j: (0, j)),
            pl.BlockSpec((D_MODEL, LANES), lambda i, j: (0, 0)),
        ],
        out_specs=[
            pl.BlockSpec((tm, tn), lambda i, j: (i, j)),
            pl.BlockSpec((tm, LANES), lambda i, j: (i, 0)),
        ],
        out_shape=[
            jax.ShapeDtypeStruct((m, D_PROJ), BF16),
            jax.ShapeDtypeStruct((m, LANES), F32),
        ],
        scratch_shapes=[pltpu.VMEM((tm, D_MODEL), BF16)],
        compiler_params=pltpu.CompilerParams(
            dimension_semantics=("parallel", "arbitrary"), vmem_limit_bytes=VMEM_LIMIT),
        name="in_proj",
    )(x2d, w_main, w_dt)


def _attn_body(slope_ref, q_ref, k_ref, v_ref, o_ref, qf, kf, vf, qb, kb, vb, acc, m_s, l_s, bias):
    hp, b, u = pl.program_id(0), pl.program_id(1), pl.program_id(2)
    first_head = _lane_is_first_head()
    cur = (u + 1) * UNIT

    @pl.when(jnp.logical_and(b == 0, u == 0))
    def _():
        row = lax.broadcasted_iota(jnp.int32, (2 * ATT_STEPS, 2 * ATT_STEPS), 0)
        col = lax.broadcasted_iota(jnp.int32, (2 * ATT_STEPS, 2 * ATT_STEPS), 1)
        step = (row & (ATT_STEPS - 1)) + ATT_STEPS - col
        valid = (step >= 0) & (step <= ATT_STEPS)
        slope = jnp.where(row < ATT_STEPS, slope_ref[2 * hp], slope_ref[2 * hp + 1])
        for di, d in enumerate(DILATIONS):
            table = jnp.where(valid, -(slope * LOG2_E) * (step * d).astype(F32), NEG)
            bias[2 * di] = table
            bias[2 * di + 1] = jnp.where(col < ATT_STEPS, NEG, table)

    @pl.when(u == 0)
    def _():
        kf[0:UNIT, :] = jnp.zeros((UNIT, LANES), F32)
        vf[0:UNIT, :] = jnp.zeros((UNIT, LANES), F32)
        kb[0:ATT_STEPS, :] = jnp.zeros((ATT_STEPS, LANES), BF16)
        vb[0:ATT_STEPS, :] = jnp.zeros((ATT_STEPS, LANES), BF16)

    @pl.when(u > 0)
    def _():
        kb[0:ATT_STEPS, :] = kb[UNIT:UNIT + ATT_STEPS, :]
        vb[0:ATT_STEPS, :] = vb[UNIT:UNIT + ATT_STEPS, :]

    q = q_ref[...]
    qb[...] = q
    qf[...] = q.astype(F32)
    kf[pl.ds(cur, UNIT), :] = k_ref[...].astype(F32)
    vf[pl.ds(cur, UNIT), :] = v_ref[...].astype(F32)
    kb[ATT_STEPS:UNIT + ATT_STEPS, :] = k_ref[...]
    vb[ATT_STEPS:UNIT + ATT_STEPS, :] = v_ref[...]

    def rows(d, start, n):
        return pl.ds(start, n) if d == 1 else pl.ds(start, n, stride=d)

    def stack_heads(q, zero):
        return jnp.concatenate([jnp.where(first_head, q, zero), jnp.where(first_head, zero, q)], axis=0)

    def load_block(d, qs):
        if d == 1:
            qs = pl.multiple_of(qs, ATT_STEPS)
            q = qb[pl.ds(qs, ATT_STEPS), :]
            window = pl.ds(qs, 2 * ATT_STEPS)
            return stack_heads(q, jnp.zeros_like(q)), kb[window, :], vb[window, :]
        window = rows(d, cur + qs - ATT_STEPS * d, 2 * ATT_STEPS)
        q2 = stack_heads(qf[rows(d, qs, ATT_STEPS), :], 0.0).astype(BF16)
        return q2, kf[window, :].astype(BF16), vf[window, :].astype(BF16)

    def block_scores(di, d, qs, first):
        q2, kk, vv = load_block(d, qs)
        s = lax.dot_general(q2, kk, (((1,), (1,)), ((), ())), preferred_element_type=F32)
        return s + bias[2 * di + first.astype(jnp.int32)], vv

    def weighted_values(p, vv):
        vext = jnp.concatenate([vv, jnp.ones_like(vv)], axis=1)
        oe = jnp.dot(p, vext, preferred_element_type=F32)
        o_blk = jnp.where(first_head, oe[:ATT_STEPS, :LANES], oe[ATT_STEPS:, :LANES])
        l_blk = jnp.where(first_head, oe[:ATT_STEPS, LANES:], oe[ATT_STEPS:, LANES:])
        return o_blk, l_blk

    def block_start(d, t):
        blk = t // d
        r = t - blk * d
        return blk * (ATT_STEPS * d) + r, jnp.logical_and(u == 0, blk == 0)

    n_sub = UNIT // ATT_STEPS
    order = tuple(reversed(range(len(DILATIONS))))

    for di in order:
        d = DILATIONS[di]
        for t in range(n_sub):
            qs, first = block_start(d, t)
            s, vv = block_scores(di, d, qs, first)
            o_blk, l_blk = weighted_values(jnp.exp2(s).astype(BF16), vv)
            rows_q = rows(d, qs, ATT_STEPS)
            if di == order[0]:
                acc[rows_q, :] = o_blk
                l_s[rows_q, :] = l_blk
            else:
                acc[rows_q, :] += o_blk
                l_s[rows_q, :] += l_blk

    den = l_s[...]
    in_range = (den >= MIN_DENOMINATOR) & (den < jnp.inf) & (jnp.abs(acc[...]) < jnp.inf)
    n_bad = jnp.max(jnp.max(jnp.where(in_range, 0.0, 1.0), axis=1, keepdims=True), axis=0, keepdims=True)

    @pl.when(n_bad[0, 0] > 0.0)
    def _():
        for di in order:
            d = DILATIONS[di]

            def body(t, carry, di=di, d=d):
                qs, first = block_start(d, t)
                s, vv = block_scores(di, d, qs, first)
                mb = jnp.max(s, axis=1, keepdims=True)
                o_blk, l_blk = weighted_values(jnp.exp2(s - mb).astype(BF16), vv)
                m_blk = jnp.where(first_head, mb[:ATT_STEPS], mb[ATT_STEPS:])
                rows_q = rows(d, qs, ATT_STEPS)
                if di == order[0]:
                    acc[rows_q, :] = o_blk
                    l_s[rows_q, :] = l_blk
                    m_s[rows_q, :] = m_blk
                else:
                    m_old = m_s[rows_q, :]
                    m_new = jnp.maximum(m_old, m_blk)
                    c_old = jnp.exp2(m_old - m_new)
                    c_blk = jnp.exp2(m_blk - m_new)
                    acc[rows_q, :] = acc[rows_q, :] * c_old + o_blk * c_blk
                    l_s[rows_q, :] = l_s[rows_q, :] * c_old + l_blk * c_blk
                    m_s[rows_q, :] = m_new
                return carry
            lax.fori_loop(0, n_sub, body, 0)

    o_ref[...] = (acc[...] / l_s[...]).astype(BF16)


def _attention(proj3, slopes):
    bsz, seq, _ = proj3.shape
    n_units = seq // UNIT
    blk = (None, UNIT, LANES)
    section = lambda k: pl.BlockSpec(blk, lambda h, b, u: (b, u, k * HEAD_PAIRS + h))
    return pl.pallas_call(
        _attn_body,
        grid=(HEAD_PAIRS, bsz, n_units),
        in_specs=[pl.BlockSpec(memory_space=pltpu.SMEM), section(0), section(1), section(2)],
        out_specs=section(0),
        out_shape=jax.ShapeDtypeStruct((bsz, seq, D_ATT), BF16),
        scratch_shapes=[
            pltpu.VMEM((UNIT, LANES), F32),
            pltpu.VMEM(((n_units + 1) * UNIT, LANES), F32),
            pltpu.VMEM(((n_units + 1) * UNIT, LANES), F32),
            pltpu.VMEM((UNIT, LANES), BF16),
            pltpu.VMEM((UNIT + ATT_STEPS, LANES), BF16),
            pltpu.VMEM((UNIT + ATT_STEPS, LANES), BF16),
            pltpu.VMEM((UNIT, LANES), F32),
            pltpu.VMEM((UNIT, LANES), F32),
            pltpu.VMEM((UNIT, LANES), F32),
            pltpu.VMEM((2 * len(DILATIONS), 2 * ATT_STEPS, 2 * ATT_STEPS), F32),
        ],
        compiler_params=pltpu.CompilerParams(
            dimension_semantics=("arbitrary", "arbitrary", "arbitrary"), vmem_limit_bytes=VMEM_LIMIT),
        name="dilated_attention",
    )(slopes, proj3, proj3, proj3)


def _split_bf16(v, parts):
    out = []
    for _ in range(parts):
        hi = v.astype(BF16)
        out.append(hi)
        v = v - hi.astype(F32)
    return out


def _ssd_decay_body(dt_ref, dtb_ref, alog_ref, src_t_ref, acs_ref, w_ref):
    ri = lax.broadcasted_iota(jnp.int32, (CHUNK, CHUNK), 0)
    ci = lax.broadcasted_iota(jnp.int32, (CHUNK, CHUNK), 1)
    triu_b = jnp.where(ri <= ci, 1.0, 0.0).astype(BF16)
    pad_rows = jnp.zeros((CHUNK - SSM_HEADS, CHUNK), F32)
    chunk_rows = [slice(g * CHUNK, (g + 1) * CHUNK) for g in range(DECAY_CHUNKS)]
    per_chunk = lambda v: jnp.concatenate([v] * DECAY_CHUNKS, axis=0)
    dt_in = jnp.concatenate([dt_ref[r, :].T[0:SSM_HEADS, :] for r in chunk_rows], axis=0) + per_chunk(dtb_ref[...])
    dt_t = jnp.maximum(dt_in, 0.0) + jnp.log1p(jnp.exp(-jnp.abs(dt_in)))
    adt_t = dt_t * per_chunk(-jnp.exp(alog_ref[...]))
    a_cs_t = sum(jnp.dot(part, triu_b, preferred_element_type=F32) for part in _split_bf16(adt_t, 3))
    w_t = dt_t * jnp.exp(a_cs_t[:, CHUNK - 1:CHUNK] - a_cs_t)
    src_t = a_cs_t - jnp.log(dt_t)
    for g, r in enumerate(chunk_rows):
        heads = slice(g * SSM_HEADS, (g + 1) * SSM_HEADS)
        src_t_ref[g] = src_t[heads, :]
        acs_ref[r, :] = jnp.concatenate([a_cs_t[heads, :], pad_rows], axis=0).T
        w_ref[r, :] = jnp.concatenate([w_t[heads, :], pad_rows], axis=0).T


def _ssd_decay(dt3, dtb, alog):
    bsz, seq, _ = dt3.shape
    n_chunks = seq // CHUNK
    rows = DECAY_CHUNKS * CHUNK
    full = lambda shape: pl.BlockSpec(shape, lambda b, i: (0,) * len(shape))
    pos_major = pl.BlockSpec((None, rows, LANES), lambda b, i: (b, i, 0))
    return pl.pallas_call(
        _ssd_decay_body,
        grid=(bsz, n_chunks // DECAY_CHUNKS),
        in_specs=[pos_major, full((SSM_HEADS, 1)), full((SSM_HEADS, 1))],
        out_specs=[
            pl.BlockSpec((None, DECAY_CHUNKS, SSM_HEADS, CHUNK), lambda b, i: (b, i, 0, 0)),
            pos_major, pos_major,
        ],
        out_shape=[
            jax.ShapeDtypeStruct((bsz, n_chunks, SSM_HEADS, CHUNK), F32),
            jax.ShapeDtypeStruct((bsz, seq, LANES), F32),
            jax.ShapeDtypeStruct((bsz, seq, LANES), F32),
        ],
        compiler_params=pltpu.CompilerParams(
            dimension_semantics=("parallel", "parallel"), vmem_limit_bytes=VMEM_LIMIT),
        name="ssd_decay",
    )(dt3, dtb, alog)


def _ssd_body(xs_ref, bc_ref, src_t_ref, acs_ref, wp_ref, sel_ref, cw_ref, cb_ref, dskip_ref, o_ref,
              upad, tail, state):
    @pl.when(pl.program_id(1) == 0)
    def _():
        upad[0:CHUNK, :] = jnp.zeros((CHUNK, D_XBC), BF16)
        tail[...] = jnp.zeros_like(tail)
        state[...] = jnp.zeros_like(state)

    upad[CHUNK - CONV_TAIL:CHUNK, :] = tail[...]
    upad[CHUNK:2 * CHUNK, 0:D_SSM] = xs_ref[...]
    upad[CHUNK:2 * CHUNK, D_SSM:D_XBC] = bc_ref[...]
    shifted = jnp.dot(sel_ref[...], upad[...], preferred_element_type=F32)
    conv = cb_ref[...] + cw_ref[CONV_WIDTH - 1:CONV_WIDTH, :] * upad[CHUNK:2 * CHUNK, :].astype(F32)
    for tap in range(CONV_WIDTH - 1):
        conv = conv + cw_ref[tap:tap + 1, :] * shifted[tap * CHUNK:(tap + 1) * CHUNK, :]
    xbc = _silu(conv)
    tail[:, 0:D_SSM] = xs_ref[CHUNK - CONV_TAIL:CHUNK, :]
    tail[:, D_SSM:D_XBC] = bc_ref[CHUNK - CONV_TAIL:CHUNK, :]

    xs = xbc[:, 0:D_SSM]
    xs_b = xs.astype(BF16)
    first_head = _lane_is_first_head()
    a_cs = acs_ref[...]
    w_p = wp_ref[...]
    src_t = src_t_ref[...]
    ri = lax.broadcasted_iota(jnp.int32, (CHUNK, CHUNK), 0)
    ci = lax.broadcasted_iota(jnp.int32, (CHUNK, CHUNK), 1)
    tril = ri >= ci

    def spread(v, pairs):
        return jnp.concatenate(
            [jnp.where(first_head, v[:, 2 * p:2 * p + 1], v[:, 2 * p + 1:2 * p + 2]) for p in pairs], axis=1)

    y_groups = []
    pairs_per_group = HEAD_PAIRS // SSM_GROUPS
    for g in range(SSM_GROUPS):
        pairs = range(g * pairs_per_group, (g + 1) * pairs_per_group)
        gcols = slice(pairs[0] * LANES, (pairs[-1] + 1) * LANES)
        bg = xbc[:, D_SSM + g * SSM_STATE:D_SSM + (g + 1) * SSM_STATE]
        cg = xbc[:, D_SSM + (SSM_GROUPS + g) * SSM_STATE:D_SSM + (SSM_GROUPS + g + 1) * SSM_STATE].astype(BF16)
        cb = lax.dot_general(cg, bg.astype(BF16), (((1,), (1,)), ((), ())), preferred_element_type=F32)
        y_diag = []
        for p in pairs:
            xs_p = xs_b[:, p * LANES:(p + 1) * LANES]
            zero = jnp.zeros_like(xs_p)
            m_pair = []
            for h in (2 * p, 2 * p + 1):
                seg = a_cs[:, h:h + 1] - src_t[h:h + 1, :]
                m_pair.append((cb * jnp.exp(jnp.where(tril, seg, NEG))).astype(BF16))
            rhs = jnp.concatenate([jnp.where(first_head, xs_p, zero), jnp.where(first_head, zero, xs_p)], axis=0)
            y_diag.append(jnp.dot(jnp.concatenate(m_pair, axis=1), rhs, preferred_element_type=F32))
        prev = state[:, gcols]
        y_off = jnp.dot(cg, prev.astype(BF16), preferred_element_type=F32)
        y_groups.append(jnp.concatenate(y_diag, axis=1) + y_off * jnp.exp(spread(a_cs, pairs)))
        xw = (xs[:, gcols] * spread(w_p, pairs)).astype(BF16)
        new = jnp.dot(bg.T.astype(BF16), xw, preferred_element_type=F32)
        state[:, gcols] = prev * jnp.exp(spread(a_cs[CHUNK - 1:CHUNK, :], pairs)) + new

    o_ref[...] = (jnp.concatenate(y_groups, axis=1) + dskip_ref[...] * xs).astype(BF16)


def _ssd(proj3, src_t, a_cs, w_p, conv_w, conv_b, dskip):
    bsz, seq, _ = proj3.shape
    sel = np.zeros(((CONV_WIDTH - 1) * CHUNK, 2 * CHUNK), np.float32)
    for tap in range(CONV_WIDTH - 1):
        sel[tap * CHUNK + np.arange(CHUNK), CHUNK - (CONV_WIDTH - 1) + tap + np.arange(CHUNK)] = 1.0
    full = lambda shape: pl.BlockSpec(shape, lambda b, c: (0,) * len(shape))
    pos_major = pl.BlockSpec((None, CHUNK, LANES), lambda b, c: (b, c, 0))
    return pl.pallas_call(
        _ssd_body,
        grid=(bsz, seq // CHUNK),
        in_specs=[
            pl.BlockSpec((None, CHUNK, D_SSM), lambda b, c: (b, c, COL_XS)),
            pl.BlockSpec((None, CHUNK, D_BC), lambda b, c: (b, c, (COL_XS + 1) * D_SSM // D_BC)),
            pl.BlockSpec((None, None, SSM_HEADS, CHUNK), lambda b, c: (b, c, 0, 0)),
            pos_major, pos_major,
            full(sel.shape), full((CONV_WIDTH, D_XBC)), full((1, D_XBC)), full((1, D_SSM)),
        ],
        out_specs=pl.BlockSpec((None, CHUNK, D_SSM), lambda b, c: (b, c, 0)),
        out_shape=jax.ShapeDtypeStruct((bsz, seq, D_SSM), BF16),
        scratch_shapes=[
            pltpu.VMEM((2 * CHUNK, D_XBC), BF16),
            pltpu.VMEM((CONV_TAIL, D_XBC), BF16),
            pltpu.VMEM((SSM_STATE, D_SSM), F32),
        ],
        compiler_params=pltpu.CompilerParams(
            dimension_semantics=("parallel", "arbitrary"), vmem_limit_bytes=VMEM_LIMIT),
        name="ssd",
    )(proj3, proj3, src_t, a_cs, w_p, jnp.asarray(sel, BF16), conv_w, conv_b, dskip)


def _out_body(alpha, ya_ref, za_ref, ys_ref, zs_ref, x_ref, w_ref, ga_ref, gs_ref, lg_ref, lb_ref, o_ref):
    def gated_rms_norm(y_ref, gate_ref, g_ref):
        y = y_ref[...].astype(F32) * gate_ref[...].astype(F32)
        return (y * lax.rsqrt(jnp.mean(y * y, axis=-1, keepdims=True) + NORM_EPS) * g_ref[...]).astype(BF16)

    mix = jnp.concatenate([gated_rms_norm(ya_ref, za_ref, ga_ref), gated_rms_norm(ys_ref, zs_ref, gs_ref)], axis=1)
    h = alpha * x_ref[...] + jnp.dot(mix, w_ref[...], preferred_element_type=F32)
    mu = jnp.mean(h, axis=-1, keepdims=True)
    hc = h - mu
    var = jnp.mean(hc * hc, axis=-1, keepdims=True)
    o_ref[...] = hc * lax.rsqrt(var + NORM_EPS) * lg_ref[...] + lb_ref[...]


def _out_proj(alpha, y_att, proj2, y_ssm, x2d, w_out, gain_att, gain_ssm, ln_g, ln_b):
    m = x2d.shape[0]
    tm = OUT_ROWS
    rows = lambda shape: pl.BlockSpec(shape, lambda i: (i, 0))
    full = lambda shape: pl.BlockSpec(shape, lambda i: (0, 0))
    return pl.pallas_call(
        functools.partial(_out_body, alpha),
        grid=(m // tm,),
        in_specs=[
            rows((tm, D_ATT)),
            pl.BlockSpec((tm, D_ATT), lambda i: (i, COL_Z_ATT)),
            rows((tm, D_SSM)),
            pl.BlockSpec((tm, D_SSM), lambda i: (i, COL_Z_SSM)),
            rows((tm, D_MODEL)),
            full((D_ATT + D_SSM, D_MODEL)),
            full((1, D_ATT)), full((1, D_SSM)), full((1, D_MODEL)), full((1, D_MODEL)),
        ],
        out_specs=rows((tm, D_MODEL)),
        out_shape=jax.ShapeDtypeStruct((m, D_MODEL), F32),
        compiler_params=pltpu.CompilerParams(
            dimension_semantics=("parallel",), vmem_limit_bytes=VMEM_LIMIT),
        name="out_proj",
    )(y_att, proj2, y_ssm, proj2, x2d, w_out, gain_att, gain_ssm, ln_g, ln_b)


def kernel(x, w_in, conv_w, conv_b, dt_bias, a_log, d_skip, att_norm_g, ssm_norm_g, w_out, ln_g, ln_b):
    bsz, seq, _ = x.shape
    depth = w_in.shape[0]
    assert seq % UNIT == 0
    alpha = (2.0 * depth) ** 0.25
    slopes = jnp.asarray(2.0 ** (-8.0 * np.arange(1, ATT_HEADS + 1) / ATT_HEADS), dtype=F32)
    for layer in range(depth):
        w_main = w_in[layer][:, :D_PROJ].astype(BF16)
        w_dt = jnp.pad(w_in[layer][:, D_PROJ:], ((0, 0), (0, LANES - SSM_HEADS))).astype(BF16)
        x2d = x.reshape(bsz * seq, D_MODEL)
        proj, dt_raw = _in_proj(x2d, w_main, w_dt)
        proj3 = proj.reshape(bsz, seq, D_PROJ)

        y_att = _attention(proj3, slopes)

        src_t, a_cs, w_p = _ssd_decay(
            dt_raw.reshape(bsz, seq, LANES),
            dt_bias[layer].astype(F32).reshape(SSM_HEADS, 1), a_log[layer].astype(F32).reshape(SSM_HEADS, 1))
        y_ssm = _ssd(
            proj3, src_t, a_cs, w_p, conv_w[layer].astype(F32), conv_b[layer].astype(F32).reshape(1, D_XBC),
            jnp.repeat(d_skip[layer].astype(F32), HEAD_DIM).reshape(1, D_SSM))

        out = _out_proj(
            alpha, y_att.reshape(bsz * seq, D_ATT), proj, y_ssm.reshape(bsz * seq, D_SSM), x2d,
            w_out[layer].astype(BF16), att_norm_g[layer].astype(F32).reshape(1, D_ATT),
            ssm_norm_g[layer].astype(F32).reshape(1, D_SSM),
            ln_g[layer].astype(F32).reshape(1, D_MODEL), ln_b[layer].astype(F32).reshape(1, D_MODEL))
        x = out.reshape(bsz, seq, D_MODEL)
    return x
```

```python
import functools

import numpy as np
import jax
import jax.numpy as jnp
from jax import lax
from jax.experimental import pallas as pl
from jax.experimental.pallas import tpu as pltpu

F32 = jnp.float32
BF16 = jnp.bfloat16

D_MODEL = 1024
D_ATT = 1024
HEAD_DIM = 64
ATT_HEADS = D_ATT // HEAD_DIM
ATT_STEPS = 128
DILATIONS = (1, 4, 16)
D_SSM = 1024
SSM_HEADS = D_SSM // HEAD_DIM
SSM_GROUPS = 2
SSM_STATE = 128
CONV_WIDTH = 4
CHUNK = 128
D_BC = 2 * SSM_GROUPS * SSM_STATE
D_PROJ = 4 * D_ATT + D_SSM + D_SSM + D_BC
D_XBC = D_SSM + D_BC
D_XBC_START = D_PROJ - D_XBC
NORM_EPS = 1e-5

LANES = 128
HEAD_PAIRS = ATT_HEADS // 2
UNIT = DILATIONS[-1] * ATT_STEPS
WIDE = DILATIONS[-1]
PADDED_GROUP = WIDE + 8
PADDED_UNIT = UNIT // WIDE * PADDED_GROUP
MIN_DENOMINATOR = 2.0 ** -100
LOG2_E = 1.4426950408889634
Q_SCALE = HEAD_DIM ** -0.5 * LOG2_E
NEG = -1e30
PROJ_ROWS, PROJ_COLS = 2048, 512
CONV_TAIL = 16
OUT_ROWS = 512
DECAY_CHUNKS = 16
VMEM_LIMIT = 56 * 1024 * 1024

COL_Z_ATT = 3
COL_Z_SSM = 4
COL_XS = 5


def _silu(v):
    h = 0.5 * v
    return h + h * jnp.tanh(h)


def _lane_is_first_head():
    return lax.broadcasted_iota(jnp.int32, (1, LANES), 1) < HEAD_DIM


def _in_proj_body(x_ref, w_ref, wdt_ref, proj_ref, dt_ref, xb_ref):
    @pl.when(pl.program_id(1) == 0)
    def _():
        xb_ref[...] = x_ref[...].astype(BF16)
        dt_ref[...] = jnp.dot(xb_ref[...], wdt_ref[...], preferred_element_type=F32)

    def project():
        return jnp.dot(xb_ref[...], w_ref[...], preferred_element_type=F32)

    j = pl.program_id(1)
    is_q = j < D_ATT // PROJ_COLS
    is_gate = jnp.logical_and(j >= COL_Z_ATT * D_ATT // PROJ_COLS, j < COL_XS * D_SSM // PROJ_COLS)

    @pl.when(is_q)
    def _():
        proj_ref[...] = (project() * Q_SCALE).astype(BF16)

    @pl.when(is_gate)
    def _():
        proj_ref[...] = _silu(project()).astype(BF16)

    @pl.when(jnp.logical_not(jnp.logical_or(is_q, is_gate)))
    def _():
        proj_ref[...] = project().astype(BF16)


def _in_proj(x2d, w_main, w_dt):
    m = x2d.shape[0]
    tm, tn = PROJ_ROWS, PROJ_COLS
    assert m % tm == 0
    return pl.pallas_call(
        _in_proj_body,
        grid=(m // tm, D_PROJ // tn),
        in_specs=[
            pl.BlockSpec((tm, D_MODEL), lambda i, j: (jnp.minimum(i + jnp.minimum(j, 1), m // tm - 1), 0)),
            pl.BlockSpec((D_MODEL, tn), lambda i, j: (0, j)),
            pl.BlockSpec((D_MODEL, LANES), lambda i, j: (0, 0)),
        ],
        out_specs=[
            pl.BlockSpec((tm, tn), lambda i, j: (i, j)),
            pl.BlockSpec((tm, LANES), lambda i, j: (i, 0)),
        ],
        out_shape=[
            jax.ShapeDtypeStruct((m, D_PROJ), BF16),
            jax.ShapeDtypeStruct((m, LANES), F32),
        ],
        scratch_shapes=[pltpu.VMEM((tm, D_MODEL), BF16)],
        compiler_params=pltpu.CompilerParams(
            dimension_semantics=("parallel", "arbitrary"), vmem_limit_bytes=VMEM_LIMIT),
        name="in_proj",
    )(x2d, w_main, w_dt)


def _attn_body(slope_ref, q_ref, k_ref, v_ref, o_ref, qf, kf, vf, qp, kp, vp, qb, kb, vb, acc, m_s, l_s, bias):
    hp, b, u = pl.program_id(0), pl.program_id(1), pl.program_id(2)
    first_head = _lane_is_first_head()
    cur = (u + 1) * UNIT
    cur_padded = (u + 1) * PADDED_UNIT

    @pl.when(jnp.logical_and(b == 0, u == 0))
    def _():
        row = lax.broadcasted_iota(jnp.int32, (2 * ATT_STEPS, 2 * ATT_STEPS), 0)
        col = lax.broadcasted_iota(jnp.int32, (2 * ATT_STEPS, 2 * ATT_STEPS), 1)
        step = (row & (ATT_STEPS - 1)) + ATT_STEPS - col
        valid = (step >= 0) & (step <= ATT_STEPS)
        slope = jnp.where(row < ATT_STEPS, slope_ref[2 * hp], slope_ref[2 * hp + 1])
        for di, d in enumerate(DILATIONS):
            table = jnp.where(valid, -(slope * LOG2_E) * (step * d).astype(F32), NEG)
            bias[2 * di] = table
            bias[2 * di + 1] = jnp.where(col < ATT_STEPS, NEG, table)

    @pl.when(u == 0)
    def _():
        kf[0:UNIT, :] = jnp.zeros((UNIT, LANES), F32)
        vf[0:UNIT, :] = jnp.zeros((UNIT, LANES), F32)
        kp[0:PADDED_UNIT, :] = jnp.zeros((PADDED_UNIT, LANES), F32)
        vp[0:PADDED_UNIT, :] = jnp.zeros((PADDED_UNIT, LANES), F32)
        kb[0:ATT_STEPS, :] = jnp.zeros((ATT_STEPS, LANES), BF16)
        vb[0:ATT_STEPS, :] = jnp.zeros((ATT_STEPS, LANES), BF16)

    @pl.when(u > 0)
    def _():
        kb[0:ATT_STEPS, :] = kb[UNIT:UNIT + ATT_STEPS, :]
        vb[0:ATT_STEPS, :] = vb[UNIT:UNIT + ATT_STEPS, :]

    q = q_ref[...]
    qb[...] = q
    q32, k32, v32 = q.astype(F32), k_ref[...].astype(F32), v_ref[...].astype(F32)
    qf[...] = q32
    kf[pl.ds(cur, UNIT), :] = k32
    vf[pl.ds(cur, UNIT), :] = v32
    for g in range(UNIT // WIDE):
        group = slice(g * WIDE, (g + 1) * WIDE)
        qp[g * PADDED_GROUP:g * PADDED_GROUP + WIDE, :] = q32[group, :]
        kp[pl.ds(cur_padded + g * PADDED_GROUP, WIDE), :] = k32[group, :]
        vp[pl.ds(cur_padded + g * PADDED_GROUP, WIDE), :] = v32[group, :]
    kb[ATT_STEPS:UNIT + ATT_STEPS, :] = k_ref[...]
    vb[ATT_STEPS:UNIT + ATT_STEPS, :] = v_ref[...]

    def rows(d, start, n):
        return pl.ds(start, n) if d == 1 else pl.ds(start, n, stride=d)

    def stack_heads(q, zero):
        return jnp.concatenate([jnp.where(first_head, q, zero), jnp.where(first_head, zero, q)], axis=0)

    def load_block(d, qs):
        if d == 1:
            qs = pl.multiple_of(qs, ATT_STEPS)
            q = qb[pl.ds(qs, ATT_STEPS), :]
            window = pl.ds(qs, 2 * ATT_STEPS)
            return stack_heads(q, jnp.zeros_like(q)), kb[window, :], vb[window, :]
        if d == WIDE:
            window = pl.ds(cur_padded - PADDED_UNIT + qs, 2 * ATT_STEPS, stride=PADDED_GROUP)
            q2 = stack_heads(qp[pl.ds(qs, ATT_STEPS, stride=PADDED_GROUP), :], 0.0).astype(BF16)
            return q2, kp[window, :].astype(BF16), vp[window, :].astype(BF16)
        window = rows(d, cur + qs - ATT_STEPS * d, 2 * ATT_STEPS)
        q2 = stack_heads(qf[rows(d, qs, ATT_STEPS), :], 0.0).astype(BF16)
        return q2, kf[window, :].astype(BF16), vf[window, :].astype(BF16)

    def block_scores(di, d, qs, first):
        q2, kk, vv = load_block(d, qs)
        s = lax.dot_general(q2, kk, (((1,), (1,)), ((), ())), preferred_element_type=F32)
        return s + bias[2 * di + first.astype(jnp.int32)], vv

    def weighted_values(p, vv):
        vext = jnp.concatenate([vv, jnp.ones_like(vv)], axis=1)
        oe = jnp.dot(p, vext, preferred_element_type=F32)
        o_blk = jnp.where(first_head, oe[:ATT_STEPS, :LANES], oe[ATT_STEPS:, :LANES])
        l_blk = jnp.where(first_head, oe[:ATT_STEPS, LANES:], oe[ATT_STEPS:, LANES:])
        return o_blk, l_blk

    def block_start(d, t):
        blk = t // d
        r = t - blk * d
        return blk * (ATT_STEPS * d) + r, jnp.logical_and(u == 0, blk == 0)

    n_sub = UNIT // ATT_STEPS
    order = tuple(reversed(range(len(DILATIONS))))

    for di in order:
        d = DILATIONS[di]
        for t in range(n_sub):
            qs, first = block_start(d, t)
            s, vv = block_scores(di, d, qs, first)
            o_blk, l_blk = weighted_values(jnp.exp2(s).astype(BF16), vv)
            rows_q = rows(d, qs, ATT_STEPS)
            if di == order[0]:
                acc[rows_q, :] = o_blk
                l_s[rows_q, :] = l_blk
            else:
                acc[rows_q, :] += o_blk
                l_s[rows_q, :] += l_blk

    den = l_s[...]
    in_range = (den >= MIN_DENOMINATOR) & (den < jnp.inf) & (jnp.abs(acc[...]) < jnp.inf)
    n_bad = jnp.max(jnp.max(jnp.where(in_range, 0.0, 1.0), axis=1, keepdims=True), axis=0, keepdims=True)

    @pl.when(n_bad[0, 0] > 0.0)
    def _():
        for di in order:
            d = DILATIONS[di]

            def body(t, carry, di=di, d=d):
                qs, first = block_start(d, t)
                s, vv = block_scores(di, d, qs, first)
                mb = jnp.max(s, axis=1, keepdims=True)
                o_blk, l_blk = weighted_values(jnp.exp2(s - mb).astype(BF16), vv)
                m_blk = jnp.where(first_head, mb[:ATT_STEPS], mb[ATT_STEPS:])
                rows_q = rows(d, qs, ATT_STEPS)
                if di == order[0]:
                    acc[rows_q, :] = o_blk
                    l_s[rows_q, :] = l_blk
                    m_s[rows_q, :] = m_blk
                else:
                    m_old = m_s[rows_q, :]
                    m_new = jnp.maximum(m_old, m_blk)
                    c_old = jnp.exp2(m_old - m_new)
                    c_blk = jnp.exp2(m_blk - m_new)
                    acc[rows_q, :] = acc[rows_q, :] * c_old + o_blk * c_blk
                    l_s[rows_q, :] = l_s[rows_q, :] * c_old + l_blk * c_blk
                    m_s[rows_q, :] = m_new
                return carry
            lax.fori_loop(0, n_sub, body, 0)

    o_ref[...] = (acc[...] / l_s[...]).astype(BF16)


def _attention(proj3, slopes):
    bsz, seq, _ = proj3.shape
    n_units = seq // UNIT
    blk = (None, UNIT, LANES)
    section = lambda k: pl.BlockSpec(blk, lambda h, b, u: (b, u, k * HEAD_PAIRS + h))
    return pl.pallas_call(
        _attn_body,
        grid=(HEAD_PAIRS, bsz, n_units),
        in_specs=[pl.BlockSpec(memory_space=pltpu.SMEM), section(0), section(1), section(2)],
        out_specs=section(0),
        out_shape=jax.ShapeDtypeStruct((bsz, seq, D_ATT), BF16),
        scratch_shapes=[
            pltpu.VMEM((UNIT, LANES), F32),
            pltpu.VMEM(((n_units + 1) * UNIT, LANES), F32),
            pltpu.VMEM(((n_units + 1) * UNIT, LANES), F32),
            pltpu.VMEM((PADDED_UNIT, LANES), F32),
            pltpu.VMEM(((n_units + 1) * PADDED_UNIT, LANES), F32),
            pltpu.VMEM(((n_units + 1) * PADDED_UNIT, LANES), F32),
            pltpu.VMEM((UNIT, LANES), BF16),
            pltpu.VMEM((UNIT + ATT_STEPS, LANES), BF16),
            pltpu.VMEM((UNIT + ATT_STEPS, LANES), BF16),
            pltpu.VMEM((UNIT, LANES), F32),
            pltpu.VMEM((UNIT, LANES), F32),
            pltpu.VMEM((UNIT, LANES), F32),
            pltpu.VMEM((2 * len(DILATIONS), 2 * ATT_STEPS, 2 * ATT_STEPS), F32),
        ],
        compiler_params=pltpu.CompilerParams(
            dimension_semantics=("arbitrary", "arbitrary", "arbitrary"), vmem_limit_bytes=VMEM_LIMIT),
        name="dilated_attention",
    )(slopes, proj3, proj3, proj3)


def _split_bf16(v, parts):
    out = []
    for _ in range(parts):
        hi = v.astype(BF16)
        out.append(hi)
        v = v - hi.astype(F32)
    return out


def _ssd_decay_body(dt_ref, dtb_ref, alog_ref, src_t_ref, acs_ref, w_ref):
    ri = lax.broadcasted_iota(jnp.int32, (CHUNK, CHUNK), 0)
    ci = lax.broadcasted_iota(jnp.int32, (CHUNK, CHUNK), 1)
    triu_b = jnp.where(ri <= ci, 1.0, 0.0).astype(BF16)
    pad_rows = jnp.zeros((CHUNK - SSM_HEADS, CHUNK), F32)
    chunk_rows = [slice(g * CHUNK, (g + 1) * CHUNK) for g in range(DECAY_CHUNKS)]
    per_chunk = lambda v: jnp.concatenate([v] * DECAY_CHUNKS, axis=0)
    dt_in = jnp.concatenate([dt_ref[r, :].T[0:SSM_HEADS, :] for r in chunk_rows], axis=0) + per_chunk(dtb_ref[...])
    dt_t = jnp.maximum(dt_in, 0.0) + jnp.log1p(jnp.exp(-jnp.abs(dt_in)))
    adt_t = dt_t * per_chunk(-jnp.exp(alog_ref[...]))
    a_cs_t = sum(jnp.dot(part, triu_b, preferred_element_type=F32) for part in _split_bf16(adt_t, 3))
    w_t = dt_t * jnp.exp(a_cs_t[:, CHUNK - 1:CHUNK] - a_cs_t)
    src_t = a_cs_t - jnp.log(dt_t)
    for g, r in enumerate(chunk_rows):
        heads = slice(g * SSM_HEADS, (g + 1) * SSM_HEADS)
        src_t_ref[g] = src_t[heads, :]
        acs_ref[r, :] = jnp.concatenate([a_cs_t[heads, :], pad_rows], axis=0).T
        w_ref[r, :] = jnp.concatenate([w_t[heads, :], pad_rows], axis=0).T


def _ssd_decay(dt3, dtb, alog):
    bsz, seq, _ = dt3.shape
    n_chunks = seq // CHUNK
    rows = DECAY_CHUNKS * CHUNK
    full = lambda shape: pl.BlockSpec(shape, lambda b, i: (0,) * len(shape))
    pos_major = pl.BlockSpec((None, rows, LANES), lambda b, i: (b, i, 0))
    return pl.pallas_call(
        _ssd_decay_body,
        grid=(bsz, n_chunks // DECAY_CHUNKS),
        in_specs=[pos_major, full((SSM_HEADS, 1)), full((SSM_HEADS, 1))],
        out_specs=[
            pl.BlockSpec((None, DECAY_CHUNKS, SSM_HEADS, CHUNK), lambda b, i: (b, i, 0, 0)),
            pos_major, pos_major,
        ],
        out_shape=[
            jax.ShapeDtypeStruct((bsz, n_chunks, SSM_HEADS, CHUNK), F32),
            jax.ShapeDtypeStruct((bsz, seq, LANES), F32),
            jax.ShapeDtypeStruct((bsz, seq, LANES), F32),
        ],
        compiler_params=pltpu.CompilerParams(
            dimension_semantics=("parallel", "parallel"), vmem_limit_bytes=VMEM_LIMIT),
        name="ssd_decay",
    )(dt3, dtb, alog)


def _ssd_body(xs_ref, bc_ref, src_t_ref, acs_ref, wp_ref, sel_ref, cw_ref, cb_ref, dskip_ref, o_ref,
              upad, tail, state):
    @pl.when(pl.program_id(1) == 0)
    def _():
        upad[0:CHUNK, :] = jnp.zeros((CHUNK, D_XBC), BF16)
        tail[...] = jnp.zeros_like(tail)
        state[...] = jnp.zeros_like(state)

    upad[CHUNK - CONV_TAIL:CHUNK, :] = tail[...]
    upad[CHUNK:2 * CHUNK, 0:D_SSM] = xs_ref[...]
    upad[CHUNK:2 * CHUNK, D_SSM:D_XBC] = bc_ref[...]
    shifted = jnp.dot(sel_ref[...], upad[...], preferred_element_type=F32)
    conv = cb_ref[...] + cw_ref[CONV_WIDTH - 1:CONV_WIDTH, :] * upad[CHUNK:2 * CHUNK, :].astype(F32)
    for tap in range(CONV_WIDTH - 1):
        conv = conv + cw_ref[tap:tap + 1, :] * shifted[tap * CHUNK:(tap + 1) * CHUNK, :]
    xbc = _silu(conv)
    tail[:, 0:D_SSM] = xs_ref[CHUNK - CONV_TAIL:CHUNK, :]
    tail[:, D_SSM:D_XBC] = bc_ref[CHUNK - CONV_TAIL:CHUNK, :]

    xs = xbc[:, 0:D_SSM]
    xs_b = xs.astype(BF16)
    first_head = _lane_is_first_head()
    a_cs = acs_ref[...]
    w_p = wp_ref[...]
    src_t = src_t_ref[...]
    ri = lax.broadcasted_iota(jnp.int32, (CHUNK, CHUNK), 0)
    ci = lax.broadcasted_iota(jnp.int32, (CHUNK, CHUNK), 1)
    tril = ri >= ci

    def spread(v, pairs):
        return jnp.concatenate(
            [jnp.where(first_head, v[:, 2 * p:2 * p + 1], v[:, 2 * p + 1:2 * p + 2]) for p in pairs], axis=1)

    y_groups = []
    pairs_per_group = HEAD_PAIRS // SSM_GROUPS
    for g in range(SSM_GROUPS):
        pairs = range(g * pairs_per_group, (g + 1) * pairs_per_group)
        gcols = slice(pairs[0] * LANES, (pairs[-1] + 1) * LANES)
        bg = xbc[:, D_SSM + g * SSM_STATE:D_SSM + (g + 1) * SSM_STATE]
        cg = xbc[:, D_SSM + (SSM_GROUPS + g) * SSM_STATE:D_SSM + (SSM_GROUPS + g + 1) * SSM_STATE].astype(BF16)
        cb = lax.dot_general(cg, bg.astype(BF16), (((1,), (1,)), ((), ())), preferred_element_type=F32)
        y_diag = []
        for p in pairs:
            xs_p = xs_b[:, p * LANES:(p + 1) * LANES]
            zero = jnp.zeros_like(xs_p)
            m_pair = []
            for h in (2 * p, 2 * p + 1):
                seg = a_cs[:, h:h + 1] - src_t[h:h + 1, :]
                m_pair.append((cb * jnp.exp(jnp.where(tril, seg, NEG))).astype(BF16))
            rhs = jnp.concatenate([jnp.where(first_head, xs_p, zero), jnp.where(first_head, zero, xs_p)], axis=0)
            y_diag.append(jnp.dot(jnp.concatenate(m_pair, axis=1), rhs, preferred_element_type=F32))
        prev = state[:, gcols]
        y_off = jnp.dot(cg, prev.astype(BF16), preferred_element_type=F32)
        y_groups.append(jnp.concatenate(y_diag, axis=1) + y_off * jnp.exp(spread(a_cs, pairs)))
        xw = (xs[:, gcols] * spread(w_p, pairs)).astype(BF16)
        new = jnp.dot(bg.T.astype(BF16), xw, preferred_element_type=F32)
        state[:, gcols] = prev * jnp.exp(spread(a_cs[CHUNK - 1:CHUNK, :], pairs)) + new

    o_ref[...] = (jnp.concatenate(y_groups, axis=1) + dskip_ref[...] * xs).astype(BF16)


def _ssd(proj3, src_t, a_cs, w_p, conv_w, conv_b, dskip):
    bsz, seq, _ = proj3.shape
    sel = np.zeros(((CONV_WIDTH - 1) * CHUNK, 2 * CHUNK), np.float32)
    for tap in range(CONV_WIDTH - 1):
        sel[tap * CHUNK + np.arange(CHUNK), CHUNK - (CONV_WIDTH - 1) + tap + np.arange(CHUNK)] = 1.0
    full = lambda shape: pl.BlockSpec(shape, lambda b, c: (0,) * len(shape))
    pos_major = pl.BlockSpec((None, CHUNK, LANES), lambda b, c: (b, c, 0))
    return pl.pallas_call(
        _ssd_body,
        grid=(bsz, seq // CHUNK),
        in_specs=[
            pl.BlockSpec((None, CHUNK, D_SSM), lambda b, c: (b, c, COL_XS)),
            pl.BlockSpec((None, CHUNK, D_BC), lambda b, c: (b, c, (COL_XS + 1) * D_SSM // D_BC)),
            pl.BlockSpec((None, None, SSM_HEADS, CHUNK), lambda b, c: (b, c, 0, 0)),
            pos_major, pos_major,
            full(sel.shape), full((CONV_WIDTH, D_XBC)), full((1, D_XBC)), full((1, D_SSM)),
        ],
        out_specs=pl.BlockSpec((None, CHUNK, D_SSM), lambda b, c: (b, c, 0)),
        out_shape=jax.ShapeDtypeStruct((bsz, seq, D_SSM), BF16),
        scratch_shapes=[
            pltpu.VMEM((2 * CHUNK, D_XBC), BF16),
            pltpu.VMEM((CONV_TAIL, D_XBC), BF16),
            pltpu.VMEM((SSM_STATE, D_SSM), F32),
        ],
        compiler_params=pltpu.CompilerParams(
            dimension_semantics=("parallel", "arbitrary"), vmem_limit_bytes=VMEM_LIMIT),
        name="ssd",
    )(proj3, proj3, src_t, a_cs, w_p, jnp.asarray(sel, BF16), conv_w, conv_b, dskip)


def _out_body(alpha, ya_ref, za_ref, ys_ref, zs_ref, x_ref, w_ref, ga_ref, gs_ref, lg_ref, lb_ref, o_ref):
    def gated_rms_norm(y_ref, gate_ref, g_ref):
        y = y_ref[...].astype(F32) * gate_ref[...].astype(F32)
        return (y * lax.rsqrt(jnp.mean(y * y, axis=-1, keepdims=True) + NORM_EPS) * g_ref[...]).astype(BF16)

    mix = jnp.concatenate([gated_rms_norm(ya_ref, za_ref, ga_ref), gated_rms_norm(ys_ref, zs_ref, gs_ref)], axis=1)
    h = alpha * x_ref[...] + jnp.dot(mix, w_ref[...], preferred_element_type=F32)
    mu = jnp.mean(h, axis=-1, keepdims=True)
    hc = h - mu
    var = jnp.mean(hc * hc, axis=-1, keepdims=True)
    o_ref[...] = hc * lax.rsqrt(var + NORM_EPS) * lg_ref[...] + lb_ref[...]


def _out_proj(alpha, y_att, proj2, y_ssm, x2d, w_out, gain_att, gain_ssm, ln_g, ln_b):
    m = x2d.shape[0]
    tm = OUT_ROWS
    rows = lambda shape: pl.BlockSpec(shape, lambda i: (i, 0))
    full = lambda shape: pl.BlockSpec(shape, lambda i: (0, 0))
    return pl.pallas_call(
        functools.partial(_out_body, alpha),
        grid=(m // tm,),
        in_specs=[
            rows((tm, D_ATT)),
            pl.BlockSpec((tm, D_ATT), lambda i: (i, COL_Z_ATT)),
            rows((tm, D_SSM)),
            pl.BlockSpec((tm, D_SSM), lambda i: (i, COL_Z_SSM)),
            rows((tm, D_MODEL)),
            full((D_ATT + D_SSM, D_MODEL)),
            full((1, D_ATT)), full((1, D_SSM)), full((1, D_MODEL)), full((1, D_MODEL)),
        ],
        out_specs=rows((tm, D_MODEL)),
        out_shape=jax.ShapeDtypeStruct((m, D_MODEL), F32),
        compiler_params=pltpu.CompilerParams(
            dimension_semantics=("parallel",), vmem_limit_bytes=VMEM_LIMIT),
        name="out_proj",
    )(y_att, proj2, y_ssm, proj2, x2d, w_out, gain_att, gain_ssm, ln_g, ln_b)


def kernel(x, w_in, conv_w, conv_b, dt_bias, a_log, d_skip, att_norm_g, ssm_norm_g, w_out, ln_g, ln_b):
    bsz, seq, _ = x.shape
    depth = w_in.shape[0]
    assert seq % UNIT == 0
    alpha = (2.0 * depth) ** 0.25
    slopes = jnp.asarray(2.0 ** (-8.0 * np.arange(1, ATT_HEADS + 1) / ATT_HEADS), dtype=F32)
    for layer in range(depth):
        w_main = w_in[layer][:, :D_PROJ].astype(BF16)
        w_dt = jnp.pad(w_in[layer][:, D_PROJ:], ((0, 0), (0, LANES - SSM_HEADS))).astype(BF16)
        x2d = x.reshape(bsz * seq, D_MODEL)
        proj, dt_raw = _in_proj(x2d, w_main, w_dt)
        proj3 = proj.reshape(bsz, seq, D_PROJ)

        y_att = _attention(proj3, slopes)

        src_t, a_cs, w_p = _ssd_decay(
            dt_raw.reshape(bsz, seq, LANES),
            dt_bias[layer].astype(F32).reshape(SSM_HEADS, 1), a_log[layer].astype(F32).reshape(SSM_HEADS, 1))
        y_ssm = _ssd(
            proj3, src_t, a_cs, w_p, conv_w[layer].astype(F32), conv_b[layer].astype(F32).reshape(1, D_XBC),
            jnp.repeat(d_skip[layer].astype(F32), HEAD_DIM).reshape(1, D_SSM))

        out = _out_proj(
            alpha, y_att.reshape(bsz * seq, D_ATT), proj, y_ssm.reshape(bsz * seq, D_SSM), x2d,
            w_out[layer].astype(BF16), att_norm_g[layer].astype(F32).reshape(1, D_ATT),
            ssm_norm_g[layer].astype(F32).reshape(1, D_SSM),
            ln_g[layer].astype(F32).reshape(1, D_MODEL), ln_b[layer].astype(F32).reshape(1, D_MODEL))
        x = out.reshape(bsz, seq, D_MODEL)
    return x
```

```python
import functools

import numpy as np
import jax
import jax.numpy as jnp
from jax import lax
from jax.experimental import pallas as pl
from jax.experimental.pallas import tpu as pltpu

F32 = jnp.float32
BF16 = jnp.bfloat16

D_MODEL = 1024
D_ATT = 1024
HEAD_DIM = 64
ATT_HEADS = D_ATT // HEAD_DIM
ATT_STEPS = 128
DILATIONS = (1, 4, 16)
D_SSM = 1024
SSM_HEADS = D_SSM // HEAD_DIM
SSM_GROUPS = 2
SSM_STATE = 128
CONV_WIDTH = 4
CHUNK = 128
D_BC = 2 * SSM_GROUPS * SSM_STATE
D_PROJ = 4 * D_ATT + D_SSM + D_SSM + D_BC
D_XBC = D_SSM + D_BC
D_XBC_START = D_PROJ - D_XBC
NORM_EPS = 1e-5

LANES = 128
HEAD_PAIRS = ATT_HEADS // 2
UNIT = DILATIONS[-1] * ATT_STEPS
WIDE = DILATIONS[-1]
PADDED_GROUP = WIDE + 8
PADDED_UNIT = UNIT // WIDE * PADDED_GROUP
MIN_DENOMINATOR = 2.0 ** -100
LOG2_E = 1.4426950408889634
Q_SCALE = HEAD_DIM ** -0.5 * LOG2_E
NEG = -1e30
PROJ_ROWS, PROJ_COLS = 2048, 512
CONV_TAIL = 16
OUT_ROWS = 512
DECAY_CHUNKS = 16
VMEM_LIMIT = 56 * 1024 * 1024

COL_Z_ATT = 3
COL_Z_SSM = 4
COL_XS = 5


def _silu(v):
    h = 0.5 * v
    return h + h * jnp.tanh(h)


def _lane_is_first_head():
    return lax.broadcasted_iota(jnp.int32, (1, LANES), 1) < HEAD_DIM


def _in_proj_body(x_ref, w_ref, wdt_ref, proj_ref, dt_ref, xb_ref):
    @pl.when(pl.program_id(1) == 0)
    def _():
        xb_ref[...] = x_ref[...].astype(BF16)
        dt_ref[...] = jnp.dot(xb_ref[...], wdt_ref[...], preferred_element_type=F32)

    def project():
        return jnp.dot(xb_ref[...], w_ref[...], preferred_element_type=F32)

    j = pl.program_id(1)
    is_q = j < D_ATT // PROJ_COLS
    is_gate = jnp.logical_and(j >= COL_Z_ATT * D_ATT // PROJ_COLS, j < COL_XS * D_SSM // PROJ_COLS)

    @pl.when(is_q)
    def _():
        proj_ref[...] = (project() * Q_SCALE).astype(BF16)

    @pl.when(is_gate)
    def _():
        proj_ref[...] = _silu(project()).astype(BF16)

    @pl.when(jnp.logical_not(jnp.logical_or(is_q, is_gate)))
    def _():
        proj_ref[...] = project().astype(BF16)


def _in_proj(x2d, w_main, w_dt):
    m = x2d.shape[0]
    tm, tn = PROJ_ROWS, PROJ_COLS
    assert m % tm == 0
    return pl.pallas_call(
        _in_proj_body,
        grid=(m // tm, D_PROJ // tn),
        in_specs=[
            pl.BlockSpec((tm, D_MODEL), lambda i, j: (jnp.minimum(i + jnp.minimum(j, 1), m // tm - 1), 0)),
            pl.BlockSpec((D_MODEL, tn), lambda i, j: (0, j)),
            pl.BlockSpec((D_MODEL, LANES), lambda i, j: (0, 0)),
        ],
        out_specs=[
            pl.BlockSpec((tm, tn), lambda i, j: (i, j)),
            pl.BlockSpec((tm, LANES), lambda i, j: (i, 0)),
        ],
        out_shape=[
            jax.ShapeDtypeStruct((m, D_PROJ), BF16),
            jax.ShapeDtypeStruct((m, LANES), F32),
        ],
        scratch_shapes=[pltpu.VMEM((tm, D_MODEL), BF16)],
        compiler_params=pltpu.CompilerParams(
            dimension_semantics=("parallel", "arbitrary"), vmem_limit_bytes=VMEM_LIMIT),
        name="in_proj",
    )(x2d, w_main, w_dt)


def _attn_body(slope_ref, q_ref, k_ref, v_ref, o_ref, qf, kf, vf, qp, kp, vp, qb, kb, vb, acc, m_s, l_s, bias):
    hp, b, u = pl.program_id(0), pl.program_id(1), pl.program_id(2)
    first_head = _lane_is_first_head()
    cur = (u + 1) * UNIT
    cur_padded = (u + 1) * PADDED_UNIT

    @pl.when(jnp.logical_and(b == 0, u == 0))
    def _():
        row = lax.broadcasted_iota(jnp.int32, (2 * ATT_STEPS, 2 * ATT_STEPS), 0)
        col = lax.broadcasted_iota(jnp.int32, (2 * ATT_STEPS, 2 * ATT_STEPS), 1)
        step = (row & (ATT_STEPS - 1)) + ATT_STEPS - col
        valid = (step >= 0) & (step <= ATT_STEPS)
        slope = jnp.where(row < ATT_STEPS, slope_ref[2 * hp], slope_ref[2 * hp + 1])
        for di, d in enumerate(DILATIONS):
            table = jnp.where(valid, -(slope * LOG2_E) * (step * d).astype(F32), NEG)
            bias[2 * di] = table
            bias[2 * di + 1] = jnp.where(col < ATT_STEPS, NEG, table)

    @pl.when(jnp.logical_and(hp == 0, jnp.logical_and(b == 0, u == 0)))
    def _():
        kf[0:UNIT, :] = jnp.zeros((UNIT, LANES), F32)
        vf[0:UNIT, :] = jnp.zeros((UNIT, LANES), F32)
        kp[0:PADDED_UNIT, :] = jnp.zeros((PADDED_UNIT, LANES), F32)
        vp[0:PADDED_UNIT, :] = jnp.zeros((PADDED_UNIT, LANES), F32)

    @pl.when(u == 0)
    def _():
        kb[0:ATT_STEPS, :] = jnp.zeros((ATT_STEPS, LANES), BF16)
        vb[0:ATT_STEPS, :] = jnp.zeros((ATT_STEPS, LANES), BF16)

    @pl.when(u > 0)
    def _():
        kb[0:ATT_STEPS, :] = kb[UNIT:UNIT + ATT_STEPS, :]
        vb[0:ATT_STEPS, :] = vb[UNIT:UNIT + ATT_STEPS, :]

    q = q_ref[...]
    qb[...] = q
    q32, k32, v32 = q.astype(F32), k_ref[...].astype(F32), v_ref[...].astype(F32)
    qf[...] = q32
    kf[pl.ds(cur, UNIT), :] = k32
    vf[pl.ds(cur, UNIT), :] = v32
    for g in range(UNIT // WIDE):
        group = slice(g * WIDE, (g + 1) * WIDE)
        qp[g * PADDED_GROUP:g * PADDED_GROUP + WIDE, :] = q32[group, :]
        kp[pl.ds(cur_padded + g * PADDED_GROUP, WIDE), :] = k32[group, :]
        vp[pl.ds(cur_padded + g * PADDED_GROUP, WIDE), :] = v32[group, :]
    kb[ATT_STEPS:UNIT + ATT_STEPS, :] = k_ref[...]
    vb[ATT_STEPS:UNIT + ATT_STEPS, :] = v_ref[...]

    def rows(d, start, n):
        return pl.ds(start, n) if d == 1 else pl.ds(start, n, stride=d)

    def stack_heads(q, zero):
        return jnp.concatenate([jnp.where(first_head, q, zero), jnp.where(first_head, zero, q)], axis=0)

    def load_block(d, qs):
        if d == 1:
            qs = pl.multiple_of(qs, ATT_STEPS)
            q = qb[pl.ds(qs, ATT_STEPS), :]
            window = pl.ds(qs, 2 * ATT_STEPS)
            return stack_heads(q, jnp.zeros_like(q)), kb[window, :], vb[window, :]
        if d == WIDE:
            window = pl.ds(cur_padded - PADDED_UNIT + qs, 2 * ATT_STEPS, stride=PADDED_GROUP)
            q2 = stack_heads(qp[pl.ds(qs, ATT_STEPS, stride=PADDED_GROUP), :], 0.0).astype(BF16)
            return q2, kp[window, :].astype(BF16), vp[window, :].astype(BF16)
        window = rows(d, cur + qs - ATT_STEPS * d, 2 * ATT_STEPS)
        q2 = stack_heads(qf[rows(d, qs, ATT_STEPS), :], 0.0).astype(BF16)
        return q2, kf[window, :].astype(BF16), vf[window, :].astype(BF16)

    def block_scores(di, d, qs, first):
        q2, kk, vv = load_block(d, qs)
        s = lax.dot_general(q2, kk, (((1,), (1,)), ((), ())), preferred_element_type=F32)
        return s + bias[2 * di + first.astype(jnp.int32)], vv

    def weighted_values(p, vv):
        vext = jnp.concatenate([vv, jnp.ones_like(vv)], axis=1)
        oe = jnp.dot(p, vext, preferred_element_type=F32)
        o_blk = jnp.where(first_head, oe[:ATT_STEPS, :LANES], oe[ATT_STEPS:, :LANES])
        l_blk = jnp.where(first_head, oe[:ATT_STEPS, LANES:], oe[ATT_STEPS:, LANES:])
        return o_blk, l_blk

    def block_start(d, t):
        blk = t // d
        r = t - blk * d
        return blk * (ATT_STEPS * d) + r, jnp.logical_and(u == 0, blk == 0)

    n_sub = UNIT // ATT_STEPS
    order = tuple(reversed(range(len(DILATIONS))))

    for di in order:
        d = DILATIONS[di]
        for t in range(n_sub):
            qs, first = block_start(d, t)
            s, vv = block_scores(di, d, qs, first)
            o_blk, l_blk = weighted_values(jnp.exp2(s).astype(BF16), vv)
            rows_q = rows(d, qs, ATT_STEPS)
            if di == order[0]:
                acc[rows_q, :] = o_blk
                l_s[rows_q, :] = l_blk
            else:
                acc[rows_q, :] += o_blk
                l_s[rows_q, :] += l_blk

    den = l_s[...]
    in_range = (den >= MIN_DENOMINATOR) & (den < jnp.inf) & (jnp.abs(acc[...]) < jnp.inf)
    n_bad = jnp.max(jnp.max(jnp.where(in_range, 0.0, 1.0), axis=1, keepdims=True), axis=0, keepdims=True)

    @pl.when(n_bad[0, 0] > 0.0)
    def _():
        for di in order:
            d = DILATIONS[di]

            def body(t, carry, di=di, d=d):
                qs, first = block_start(d, t)
                s, vv = block_scores(di, d, qs, first)
                mb = jnp.max(s, axis=1, keepdims=True)
                o_blk, l_blk = weighted_values(jnp.exp2(s - mb).astype(BF16), vv)
                m_blk = jnp.where(first_head, mb[:ATT_STEPS], mb[ATT_STEPS:])
                rows_q = rows(d, qs, ATT_STEPS)
                if di == order[0]:
                    acc[rows_q, :] = o_blk
                    l_s[rows_q, :] = l_blk
                    m_s[rows_q, :] = m_blk
                else:
                    m_old = m_s[rows_q, :]
                    m_new = jnp.maximum(m_old, m_blk)
                    c_old = jnp.exp2(m_old - m_new)
                    c_blk = jnp.exp2(m_blk - m_new)
                    acc[rows_q, :] = acc[rows_q, :] * c_old + o_blk * c_blk
                    l_s[rows_q, :] = l_s[rows_q, :] * c_old + l_blk * c_blk
                    m_s[rows_q, :] = m_new
                return carry
            lax.fori_loop(0, n_sub, body, 0)

    o_ref[...] = (acc[...] / l_s[...]).astype(BF16)


def _attention(proj3, slopes):
    bsz, seq, _ = proj3.shape
    n_units = seq // UNIT
    blk = (None, UNIT, LANES)
    section = lambda k: pl.BlockSpec(blk, lambda h, b, u: (b, u, k * HEAD_PAIRS + h))
    return pl.pallas_call(
        _attn_body,
        grid=(HEAD_PAIRS, bsz, n_units),
        in_specs=[pl.BlockSpec(memory_space=pltpu.SMEM), section(0), section(1), section(2)],
        out_specs=section(0),
        out_shape=jax.ShapeDtypeStruct((bsz, seq, D_ATT), BF16),
        scratch_shapes=[
            pltpu.VMEM((UNIT, LANES), F32),
            pltpu.VMEM(((n_units + 1) * UNIT, LANES), F32),
            pltpu.VMEM(((n_units + 1) * UNIT, LANES), F32),
            pltpu.VMEM((PADDED_UNIT, LANES), F32),
            pltpu.VMEM(((n_units + 1) * PADDED_UNIT, LANES), F32),
            pltpu.VMEM(((n_units + 1) * PADDED_UNIT, LANES), F32),
            pltpu.VMEM((UNIT, LANES), BF16),
            pltpu.VMEM((UNIT + ATT_STEPS, LANES), BF16),
            pltpu.VMEM((UNIT + ATT_STEPS, LANES), BF16),
            pltpu.VMEM((UNIT, LANES), F32),
            pltpu.VMEM((UNIT, LANES), F32),
            pltpu.VMEM((UNIT, LANES), F32),
            pltpu.VMEM((2 * len(DILATIONS), 2 * ATT_STEPS, 2 * ATT_STEPS), F32),
        ],
        compiler_params=pltpu.CompilerParams(
            dimension_semantics=("arbitrary", "arbitrary", "arbitrary"), vmem_limit_bytes=VMEM_LIMIT),
        name="dilated_attention",
    )(slopes, proj3, proj3, proj3)


def _split_bf16(v, parts):
    out = []
    for _ in range(parts):
        hi = v.astype(BF16)
        out.append(hi)
        v = v - hi.astype(F32)
    return out


def _ssd_decay_body(dt_ref, dtb_ref, alog_ref, src_t_ref, acs_ref, w_ref):
    ri = lax.broadcasted_iota(jnp.int32, (CHUNK, CHUNK), 0)
    ci = lax.broadcasted_iota(jnp.int32, (CHUNK, CHUNK), 1)
    triu_b = jnp.where(ri <= ci, 1.0, 0.0).astype(BF16)
    pad_rows = jnp.zeros((CHUNK - SSM_HEADS, CHUNK), F32)
    chunk_rows = [slice(g * CHUNK, (g + 1) * CHUNK) for g in range(DECAY_CHUNKS)]
    per_chunk = lambda v: jnp.concatenate([v] * DECAY_CHUNKS, axis=0)
    dt_in = jnp.concatenate([dt_ref[r, :].T[0:SSM_HEADS, :] for r in chunk_rows], axis=0) + per_chunk(dtb_ref[...])
    dt_t = jnp.maximum(dt_in, 0.0) + jnp.log1p(jnp.exp(-jnp.abs(dt_in)))
    adt_t = dt_t * per_chunk(-jnp.exp(alog_ref[...]))
    a_cs_t = sum(jnp.dot(part, triu_b, preferred_element_type=F32) for part in _split_bf16(adt_t, 3))
    w_t = dt_t * jnp.exp(a_cs_t[:, CHUNK - 1:CHUNK] - a_cs_t)
    src_t = a_cs_t - jnp.log(dt_t)
    for g, r in enumerate(chunk_rows):
        heads = slice(g * SSM_HEADS, (g + 1) * SSM_HEADS)
        src_t_ref[g] = src_t[heads, :]
        acs_ref[r, :] = jnp.concatenate([a_cs_t[heads, :], pad_rows], axis=0).T
        w_ref[r, :] = jnp.concatenate([w_t[heads, :], pad_rows], axis=0).T


def _ssd_decay(dt3, dtb, alog):
    bsz, seq, _ = dt3.shape
    n_chunks = seq // CHUNK
    rows = DECAY_CHUNKS * CHUNK
    full = lambda shape: pl.BlockSpec(shape, lambda b, i: (0,) * len(shape))
    pos_major = pl.BlockSpec((None, rows, LANES), lambda b, i: (b, i, 0))
    return pl.pallas_call(
        _ssd_decay_body,
        grid=(bsz, n_chunks // DECAY_CHUNKS),
        in_specs=[pos_major, full((SSM_HEADS, 1)), full((SSM_HEADS, 1))],
        out_specs=[
            pl.BlockSpec((None, DECAY_CHUNKS, SSM_HEADS, CHUNK), lambda b, i: (b, i, 0, 0)),
            pos_major, pos_major,
        ],
        out_shape=[
            jax.ShapeDtypeStruct((bsz, n_chunks, SSM_HEADS, CHUNK), F32),
            jax.ShapeDtypeStruct((bsz, seq, LANES), F32),
            jax.ShapeDtypeStruct((bsz, seq, LANES), F32),
        ],
        compiler_params=pltpu.CompilerParams(
            dimension_semantics=("parallel", "parallel"), vmem_limit_bytes=VMEM_LIMIT),
        name="ssd_decay",
    )(dt3, dtb, alog)


def _ssd_body(xs_ref, bc_ref, src_t_ref, acs_ref, wp_ref, sel_ref, cw_ref, cb_ref, dskip_ref, o_ref,
              upad, tail, state):
    @pl.when(pl.program_id(1) == 0)
    def _():
        upad[0:CHUNK, :] = jnp.zeros((CHUNK, D_XBC), BF16)
        tail[...] = jnp.zeros_like(tail)
        state[...] = jnp.zeros_like(state)

    upad[CHUNK - CONV_TAIL:CHUNK, :] = tail[...]
    upad[CHUNK:2 * CHUNK, 0:D_SSM] = xs_ref[...]
    upad[CHUNK:2 * CHUNK, D_SSM:D_XBC] = bc_ref[...]
    shifted = jnp.dot(sel_ref[...], upad[...], preferred_element_type=F32)
    conv = cb_ref[...] + cw_ref[CONV_WIDTH - 1:CONV_WIDTH, :] * upad[CHUNK:2 * CHUNK, :].astype(F32)
    for tap in range(CONV_WIDTH - 1):
        conv = conv + cw_ref[tap:tap + 1, :] * shifted[tap * CHUNK:(tap + 1) * CHUNK, :]
    xbc = _silu(conv)
    tail[:, 0:D_SSM] = xs_ref[CHUNK - CONV_TAIL:CHUNK, :]
    tail[:, D_SSM:D_XBC] = bc_ref[CHUNK - CONV_TAIL:CHUNK, :]

    xs = xbc[:, 0:D_SSM]
    xs_b = xs.astype(BF16)
    first_head = _lane_is_first_head()
    a_cs = acs_ref[...]
    w_p = wp_ref[...]
    src_t = src_t_ref[...]
    ri = lax.broadcasted_iota(jnp.int32, (CHUNK, CHUNK), 0)
    ci = lax.broadcasted_iota(jnp.int32, (CHUNK, CHUNK), 1)
    tril = ri >= ci

    def spread(v, pairs):
        return jnp.concatenate(
            [jnp.where(first_head, v[:, 2 * p:2 * p + 1], v[:, 2 * p + 1:2 * p + 2]) for p in pairs], axis=1)

    y_groups = []
    pairs_per_group = HEAD_PAIRS // SSM_GROUPS
    for g in range(SSM_GROUPS):
        pairs = range(g * pairs_per_group, (g + 1) * pairs_per_group)
        gcols = slice(pairs[0] * LANES, (pairs[-1] + 1) * LANES)
        bg = xbc[:, D_SSM + g * SSM_STATE:D_SSM + (g + 1) * SSM_STATE]
        cg = xbc[:, D_SSM + (SSM_GROUPS + g) * SSM_STATE:D_SSM + (SSM_GROUPS + g + 1) * SSM_STATE].astype(BF16)
        cb = lax.dot_general(cg, bg.astype(BF16), (((1,), (1,)), ((), ())), preferred_element_type=F32)
        y_diag = []
        for p in pairs:
            xs_p = xs_b[:, p * LANES:(p + 1) * LANES]
            zero = jnp.zeros_like(xs_p)
            m_pair = []
            for h in (2 * p, 2 * p + 1):
                seg = a_cs[:, h:h + 1] - src_t[h:h + 1, :]
                m_pair.append((cb * jnp.exp(jnp.where(tril, seg, NEG))).astype(BF16))
            rhs = jnp.concatenate([jnp.where(first_head, xs_p, zero), jnp.where(first_head, zero, xs_p)], axis=0)
            y_diag.append(jnp.dot(jnp.concatenate(m_pair, axis=1), rhs, preferred_element_type=F32))
        prev = state[:, gcols]
        y_off = jnp.dot(cg, prev.astype(BF16), preferred_element_type=F32)
        y_groups.append(jnp.concatenate(y_diag, axis=1) + y_off * jnp.exp(spread(a_cs, pairs)))
        xw = (xs[:, gcols] * spread(w_p, pairs)).astype(BF16)
        new = jnp.dot(bg.T.astype(BF16), xw, preferred_element_type=F32)
        state[:, gcols] = prev * jnp.exp(spread(a_cs[CHUNK - 1:CHUNK, :], pairs)) + new

    o_ref[...] = (jnp.concatenate(y_groups, axis=1) + dskip_ref[...] * xs).astype(BF16)


def _ssd(proj3, src_t, a_cs, w_p, conv_w, conv_b, dskip):
    bsz, seq, _ = proj3.shape
    sel = np.zeros(((CONV_WIDTH - 1) * CHUNK, 2 * CHUNK), np.float32)
    for tap in range(CONV_WIDTH - 1):
        sel[tap * CHUNK + np.arange(CHUNK), CHUNK - (CONV_WIDTH - 1) + tap + np.arange(CHUNK)] = 1.0
    full = lambda shape: pl.BlockSpec(shape, lambda b, c: (0,) * len(shape))
    pos_major = pl.BlockSpec((None, CHUNK, LANES), lambda b, c: (b, c, 0))
    return pl.pallas_call(
        _ssd_body,
        grid=(bsz, seq // CHUNK),
        in_specs=[
            pl.BlockSpec((None, CHUNK, D_SSM), lambda b, c: (b, c, COL_XS)),
            pl.BlockSpec((None, CHUNK, D_BC), lambda b, c: (b, c, (COL_XS + 1) * D_SSM // D_BC)),
            pl.BlockSpec((None, None, SSM_HEADS, CHUNK), lambda b, c: (b, c, 0, 0)),
            pos_major, pos_major,
            full(sel.shape), full((CONV_WIDTH, D_XBC)), full((1, D_XBC)), full((1, D_SSM)),
        ],
        out_specs=pl.BlockSpec((None, CHUNK, D_SSM), lambda b, c: (b, c, 0)),
        out_shape=jax.ShapeDtypeStruct((bsz, seq, D_SSM), BF16),
        scratch_shapes=[
            pltpu.VMEM((2 * CHUNK, D_XBC), BF16),
            pltpu.VMEM((CONV_TAIL, D_XBC), BF16),
            pltpu.VMEM((SSM_STATE, D_SSM), F32),
        ],
        compiler_params=pltpu.CompilerParams(
            dimension_semantics=("parallel", "arbitrary"), vmem_limit_bytes=VMEM_LIMIT),
        name="ssd",
    )(proj3, proj3, src_t, a_cs, w_p, jnp.asarray(sel, BF16), conv_w, conv_b, dskip)


def _out_body(alpha, ya_ref, za_ref, ys_ref, zs_ref, x_ref, w_ref, ga_ref, gs_ref, lg_ref, lb_ref, o_ref):
    def gated_rms_norm(y_ref, gate_ref, g_ref):
        y = y_ref[...].astype(F32) * gate_ref[...].astype(F32)
        return (y * lax.rsqrt(jnp.mean(y * y, axis=-1, keepdims=True) + NORM_EPS) * g_ref[...]).astype(BF16)

    mix = jnp.concatenate([gated_rms_norm(ya_ref, za_ref, ga_ref), gated_rms_norm(ys_ref, zs_ref, gs_ref)], axis=1)
    h = alpha * x_ref[...] + jnp.dot(mix, w_ref[...], preferred_element_type=F32)
    mu = jnp.mean(h, axis=-1, keepdims=True)
    hc = h - mu
    var = jnp.mean(hc * hc, axis=-1, keepdims=True)
    o_ref[...] = hc * lax.rsqrt(var + NORM_EPS) * lg_ref[...] + lb_ref[...]


def _out_proj(alpha, y_att, proj2, y_ssm, x2d, w_out, gain_att, gain_ssm, ln_g, ln_b):
    m = x2d.shape[0]
    tm = OUT_ROWS
    rows = lambda shape: pl.BlockSpec(shape, lambda i: (i, 0))
    full = lambda shape: pl.BlockSpec(shape, lambda i: (0, 0))
    return pl.pallas_call(
        functools.partial(_out_body, alpha),
        grid=(m // tm,),
        in_specs=[
            rows((tm, D_ATT)),
            pl.BlockSpec((tm, D_ATT), lambda i: (i, COL_Z_ATT)),
            rows((tm, D_SSM)),
            pl.BlockSpec((tm, D_SSM), lambda i: (i, COL_Z_SSM)),
            rows((tm, D_MODEL)),
            full((D_ATT + D_SSM, D_MODEL)),
            full((1, D_ATT)), full((1, D_SSM)), full((1, D_MODEL)), full((1, D_MODEL)),
        ],
        out_specs=rows((tm, D_MODEL)),
        out_shape=jax.ShapeDtypeStruct((m, D_MODEL), F32),
        compiler_params=pltpu.CompilerParams(
            dimension_semantics=("parallel",), vmem_limit_bytes=VMEM_LIMIT),
        name="out_proj",
    )(y_att, proj2, y_ssm, proj2, x2d, w_out, gain_att, gain_ssm, ln_g, ln_b)


def kernel(x, w_in, conv_w, conv_b, dt_bias, a_log, d_skip, att_norm_g, ssm_norm_g, w_out, ln_g, ln_b):
    bsz, seq, _ = x.shape
    depth = w_in.shape[0]
    assert seq % UNIT == 0
    alpha = (2.0 * depth) ** 0.25
    slopes = jnp.asarray(2.0 ** (-8.0 * np.arange(1, ATT_HEADS + 1) / ATT_HEADS), dtype=F32)
    for layer in range(depth):
        w_main = w_in[layer][:, :D_PROJ].astype(BF16)
        w_dt = jnp.pad(w_in[layer][:, D_PROJ:], ((0, 0), (0, LANES - SSM_HEADS))).astype(BF16)
        x2d = x.reshape(bsz * seq, D_MODEL)
        proj, dt_raw = _in_proj(x2d, w_main, w_dt)
        proj3 = proj.reshape(bsz, seq, D_PROJ)

        y_att = _attention(proj3, slopes)

        src_t, a_cs, w_p = _ssd_decay(
            dt_raw.reshape(bsz, seq, LANES),
            dt_bias[layer].astype(F32).reshape(SSM_HEADS, 1), a_log[layer].astype(F32).reshape(SSM_HEADS, 1))
        y_ssm = _ssd(
            proj3, src_t, a_cs, w_p, conv_w[layer].astype(F32), conv_b[layer].astype(F32).reshape(1, D_XBC),
            jnp.repeat(d_skip[layer].astype(F32), HEAD_DIM).reshape(1, D_SSM))

        out = _out_proj(
            alpha, y_att.reshape(bsz * seq, D_ATT), proj, y_ssm.reshape(bsz * seq, D_SSM), x2d,
            w_out[layer].astype(BF16), att_norm_g[layer].astype(F32).reshape(1, D_ATT),
            ssm_norm_g[layer].astype(F32).reshape(1, D_SSM),
            ln_g[layer].astype(F32).reshape(1, D_MODEL), ln_b[layer].astype(F32).reshape(1, D_MODEL))
        x = out.reshape(bsz, seq, D_MODEL)
    return x
```

```python
import functools

import numpy as np
import jax
import jax.numpy as jnp
from jax import lax
from jax.experimental import pallas as pl
from jax.experimental.pallas import tpu as pltpu

F32 = jnp.float32
BF16 = jnp.bfloat16

D_MODEL = 1024
D_ATT = 1024
HEAD_DIM = 64
ATT_HEADS = D_ATT // HEAD_DIM
ATT_STEPS = 128
DILATIONS = (1, 4, 16)
D_SSM = 1024
SSM_HEADS = D_SSM // HEAD_DIM
SSM_GROUPS = 2
SSM_STATE = 128
CONV_WIDTH = 4
CHUNK = 128
D_BC = 2 * SSM_GROUPS * SSM_STATE
D_PROJ = 4 * D_ATT + D_SSM + D_SSM + D_BC
D_XBC = D_SSM + D_BC
D_XBC_START = D_PROJ - D_XBC
NORM_EPS = 1e-5

LANES = 128
HEAD_PAIRS = ATT_HEADS // 2
UNIT = DILATIONS[-1] * ATT_STEPS
WIDE = DILATIONS[-1]
PADDED_GROUP = WIDE + 8
PADDED_UNIT = UNIT // WIDE * PADDED_GROUP
MIN_DENOMINATOR = 2.0 ** -100
LOG2_E = 1.4426950408889634
Q_SCALE = HEAD_DIM ** -0.5 * LOG2_E
NEG = -1e30
PROJ_ROWS, PROJ_COLS = 2048, 512
CONV_TAIL = 16
OUT_ROWS = 512
DECAY_CHUNKS = 16
VMEM_LIMIT = 56 * 1024 * 1024

COL_Z_ATT = 3
COL_Z_SSM = 4
COL_XS = 5


def _silu(v):
    h = 0.5 * v
    return h + h * jnp.tanh(h)


def _lane_is_first_head():
    return lax.broadcasted_iota(jnp.int32, (1, LANES), 1) < HEAD_DIM


def _in_proj_body(x_ref, w_ref, wdt_ref, proj_ref, dt_ref, xb_ref):
    @pl.when(pl.program_id(1) == 0)
    def _():
        xb_ref[...] = x_ref[...].astype(BF16)
        dt_ref[...] = jnp.dot(xb_ref[...], wdt_ref[...], preferred_element_type=F32)

    def project():
        return jnp.dot(xb_ref[...], w_ref[...], preferred_element_type=F32)

    j = pl.program_id(1)
    is_q = j < D_ATT // PROJ_COLS
    is_gate = jnp.logical_and(j >= COL_Z_ATT * D_ATT // PROJ_COLS, j < COL_XS * D_SSM // PROJ_COLS)

    @pl.when(is_q)
    def _():
        proj_ref[...] = (project() * Q_SCALE).astype(BF16)

    @pl.when(is_gate)
    def _():
        proj_ref[...] = _silu(project()).astype(BF16)

    @pl.when(jnp.logical_not(jnp.logical_or(is_q, is_gate)))
    def _():
        proj_ref[...] = project().astype(BF16)


def _in_proj(x2d, w_main, w_dt):
    m = x2d.shape[0]
    tm, tn = PROJ_ROWS, PROJ_COLS
    assert m % tm == 0
    return pl.pallas_call(
        _in_proj_body,
        grid=(m // tm, D_PROJ // tn),
        in_specs=[
            pl.BlockSpec((tm, D_MODEL), lambda i, j: (jnp.minimum(i + jnp.minimum(j, 1), m // tm - 1), 0)),
            pl.BlockSpec((D_MODEL, tn), lambda i, j: (0, j)),
            pl.BlockSpec((D_MODEL, LANES), lambda i, j: (0, 0)),
        ],
        out_specs=[
            pl.BlockSpec((tm, tn), lambda i, j: (i, j)),
            pl.BlockSpec((tm, LANES), lambda i, j: (i, 0)),
        ],
        out_shape=[
            jax.ShapeDtypeStruct((m, D_PROJ), BF16),
            jax.ShapeDtypeStruct((m, LANES), F32),
        ],
        scratch_shapes=[pltpu.VMEM((tm, D_MODEL), BF16)],
        compiler_params=pltpu.CompilerParams(
            dimension_semantics=("parallel", "arbitrary"), vmem_limit_bytes=VMEM_LIMIT),
        name="in_proj",
    )(x2d, w_main, w_dt)


def _attn_body(slope_ref, q_ref, k_ref, v_ref, o_ref, qf, kf, vf, qp, kp, vp, acc, m_s, l_s, bias):
    hp, b, u = pl.program_id(0), pl.program_id(1), pl.program_id(2)
    first_head = _lane_is_first_head()
    cur = (u + 1) * UNIT
    cur_padded = (u + 1) * PADDED_UNIT

    @pl.when(jnp.logical_and(b == 0, u == 0))
    def _():
        row = lax.broadcasted_iota(jnp.int32, (2 * ATT_STEPS, 2 * ATT_STEPS), 0)
        col = lax.broadcasted_iota(jnp.int32, (2 * ATT_STEPS, 2 * ATT_STEPS), 1)
        step = (row & (ATT_STEPS - 1)) + ATT_STEPS - col
        valid = (step >= 0) & (step <= ATT_STEPS)
        slope = jnp.where(row < ATT_STEPS, slope_ref[2 * hp], slope_ref[2 * hp + 1])
        for di, d in enumerate(DILATIONS):
            table = jnp.where(valid, -(slope * LOG2_E) * (step * d).astype(F32), NEG)
            bias[2 * di] = table
            bias[2 * di + 1] = jnp.where(col < ATT_STEPS, NEG, table)

    @pl.when(jnp.logical_and(hp == 0, jnp.logical_and(b == 0, u == 0)))
    def _():
        kf[0:UNIT, :] = jnp.zeros((UNIT, LANES), F32)
        vf[0:UNIT, :] = jnp.zeros((UNIT, LANES), F32)
        kp[0:PADDED_UNIT, :] = jnp.zeros((PADDED_UNIT, LANES), F32)
        vp[0:PADDED_UNIT, :] = jnp.zeros((PADDED_UNIT, LANES), F32)

    q32, k32, v32 = q_ref[...].astype(F32), k_ref[...].astype(F32), v_ref[...].astype(F32)
    qf[...] = q32
    kf[pl.ds(cur, UNIT), :] = k32
    vf[pl.ds(cur, UNIT), :] = v32
    for g in range(UNIT // WIDE):
        group = slice(g * WIDE, (g + 1) * WIDE)
        qp[g * PADDED_GROUP:g * PADDED_GROUP + WIDE, :] = q32[group, :]
        kp[pl.ds(cur_padded + g * PADDED_GROUP, WIDE), :] = k32[group, :]
        vp[pl.ds(cur_padded + g * PADDED_GROUP, WIDE), :] = v32[group, :]

    def rows(d, start, n):
        return pl.ds(start, n) if d == 1 else pl.ds(start, n, stride=d)

    def stack_heads(q, zero):
        return jnp.concatenate([jnp.where(first_head, q, zero), jnp.where(first_head, zero, q)], axis=0)

    def load_block(d, qs):
        if d == 1:
            def reaching_back(x_ref, xf):
                before = xf[pl.ds(cur - ATT_STEPS, ATT_STEPS), :].astype(BF16)
                return jnp.concatenate([before, x_ref[0:ATT_STEPS, :]], axis=0)

            if isinstance(qs, int):
                q = q_ref[qs:qs + ATT_STEPS, :]
                if qs == 0:
                    kk, vv = reaching_back(k_ref, kf), reaching_back(v_ref, vf)
                else:
                    kk, vv = (x_ref[qs - ATT_STEPS:qs + ATT_STEPS, :] for x_ref in (k_ref, v_ref))
            else:
                q = q_ref[pl.ds(pl.multiple_of(qs, ATT_STEPS), ATT_STEPS), :]
                inside = pl.ds(pl.multiple_of(jnp.maximum(qs - ATT_STEPS, 0), ATT_STEPS), 2 * ATT_STEPS)
                kk = jnp.where(qs == 0, reaching_back(k_ref, kf), k_ref[inside, :])
                vv = jnp.where(qs == 0, reaching_back(v_ref, vf), v_ref[inside, :])
            return stack_heads(q, jnp.zeros_like(q)), kk, vv
        if d == WIDE:
            window = pl.ds(cur_padded - PADDED_UNIT + qs, 2 * ATT_STEPS, stride=PADDED_GROUP)
            q2 = stack_heads(qp[pl.ds(qs, ATT_STEPS, stride=PADDED_GROUP), :], 0.0).astype(BF16)
            return q2, kp[window, :].astype(BF16), vp[window, :].astype(BF16)
        window = rows(d, cur + qs - ATT_STEPS * d, 2 * ATT_STEPS)
        q2 = stack_heads(qf[rows(d, qs, ATT_STEPS), :], 0.0).astype(BF16)
        return q2, kf[window, :].astype(BF16), vf[window, :].astype(BF16)

    def block_scores(di, d, qs, first):
        q2, kk, vv = load_block(d, qs)
        s = lax.dot_general(q2, kk, (((1,), (1,)), ((), ())), preferred_element_type=F32)
        return s + bias[2 * di + first.astype(jnp.int32)], vv

    def weighted_values(p, vv):
        vext = jnp.concatenate([vv, jnp.ones_like(vv)], axis=1)
        oe = jnp.dot(p, vext, preferred_element_type=F32)
        o_blk = jnp.where(first_head, oe[:ATT_STEPS, :LANES], oe[ATT_STEPS:, :LANES])
        l_blk = jnp.where(first_head, oe[:ATT_STEPS, LANES:], oe[ATT_STEPS:, LANES:])
        return o_blk, l_blk

    def block_start(d, t):
        blk = t // d
        r = t - blk * d
        return blk * (ATT_STEPS * d) + r, jnp.logical_and(u == 0, blk == 0)

    n_sub = UNIT // ATT_STEPS
    order = tuple(reversed(range(len(DILATIONS))))

    for di in order:
        d = DILATIONS[di]
        for t in range(n_sub):
            qs, first = block_start(d, t)
            s, vv = block_scores(di, d, qs, first)
            o_blk, l_blk = weighted_values(jnp.exp2(s).astype(BF16), vv)
            rows_q = rows(d, qs, ATT_STEPS)
            if di == order[0]:
                acc[rows_q, :] = o_blk
                l_s[rows_q, :] = l_blk
            else:
                acc[rows_q, :] += o_blk
                l_s[rows_q, :] += l_blk

    den = l_s[...]
    in_range = (den >= MIN_DENOMINATOR) & (den < jnp.inf) & (jnp.abs(acc[...]) < jnp.inf)
    n_bad = jnp.max(jnp.max(jnp.where(in_range, 0.0, 1.0), axis=1, keepdims=True), axis=0, keepdims=True)

    @pl.when(n_bad[0, 0] > 0.0)
    def _():
        for di in order:
            d = DILATIONS[di]

            def body(t, carry, di=di, d=d):
                qs, first = block_start(d, t)
                s, vv = block_scores(di, d, qs, first)
                mb = jnp.max(s, axis=1, keepdims=True)
                o_blk, l_blk = weighted_values(jnp.exp2(s - mb).astype(BF16), vv)
                m_blk = jnp.where(first_head, mb[:ATT_STEPS], mb[ATT_STEPS:])
                rows_q = rows(d, qs, ATT_STEPS)
                if di == order[0]:
                    acc[rows_q, :] = o_blk
                    l_s[rows_q, :] = l_blk
                    m_s[rows_q, :] = m_blk
                else:
                    m_old = m_s[rows_q, :]
                    m_new = jnp.maximum(m_old, m_blk)
                    c_old = jnp.exp2(m_old - m_new)
                    c_blk = jnp.exp2(m_blk - m_new)
                    acc[rows_q, :] = acc[rows_q, :] * c_old + o_blk * c_blk
                    l_s[rows_q, :] = l_s[rows_q, :] * c_old + l_blk * c_blk
                    m_s[rows_q, :] = m_new
                return carry
            lax.fori_loop(0, n_sub, body, 0)

    o_ref[...] = (acc[...] / l_s[...]).astype(BF16)


def _attention(proj3, slopes):
    bsz, seq, _ = proj3.shape
    n_units = seq // UNIT
    blk = (None, UNIT, LANES)
    section = lambda k: pl.BlockSpec(blk, lambda h, b, u: (b, u, k * HEAD_PAIRS + h))
    return pl.pallas_call(
        _attn_body,
        grid=(HEAD_PAIRS, bsz, n_units),
        in_specs=[pl.BlockSpec(memory_space=pltpu.SMEM), section(0), section(1), section(2)],
        out_specs=section(0),
        out_shape=jax.ShapeDtypeStruct((bsz, seq, D_ATT), BF16),
        scratch_shapes=[
            pltpu.VMEM((UNIT, LANES), F32),
            pltpu.VMEM(((n_units + 1) * UNIT, LANES), F32),
            pltpu.VMEM(((n_units + 1) * UNIT, LANES), F32),
            pltpu.VMEM((PADDED_UNIT, LANES), F32),
            pltpu.VMEM(((n_units + 1) * PADDED_UNIT, LANES), F32),
            pltpu.VMEM(((n_units + 1) * PADDED_UNIT, LANES), F32),
            pltpu.VMEM((UNIT, LANES), F32),
            pltpu.VMEM((UNIT, LANES), F32),
            pltpu.VMEM((UNIT, LANES), F32),
            pltpu.VMEM((2 * len(DILATIONS), 2 * ATT_STEPS, 2 * ATT_STEPS), F32),
        ],
        compiler_params=pltpu.CompilerParams(
            dimension_semantics=("arbitrary", "arbitrary", "arbitrary"), vmem_limit_bytes=VMEM_LIMIT),
        name="dilated_attention",
    )(slopes, proj3, proj3, proj3)


def _split_bf16(v, parts):
    out = []
    for _ in range(parts):
        hi = v.astype(BF16)
        out.append(hi)
        v = v - hi.astype(F32)
    return out


def _ssd_decay_body(dt_ref, dtb_ref, alog_ref, src_t_ref, acs_ref, w_ref):
    ri = lax.broadcasted_iota(jnp.int32, (CHUNK, CHUNK), 0)
    ci = lax.broadcasted_iota(jnp.int32, (CHUNK, CHUNK), 1)
    triu_b = jnp.where(ri <= ci, 1.0, 0.0).astype(BF16)
    pad_rows = jnp.zeros((CHUNK - SSM_HEADS, CHUNK), F32)
    chunk_rows = [slice(g * CHUNK, (g + 1) * CHUNK) for g in range(DECAY_CHUNKS)]
    per_chunk = lambda v: jnp.concatenate([v] * DECAY_CHUNKS, axis=0)
    dt_in = jnp.concatenate([dt_ref[r, :].T[0:SSM_HEADS, :] for r in chunk_rows], axis=0) + per_chunk(dtb_ref[...])
    dt_t = jnp.maximum(dt_in, 0.0) + jnp.log1p(jnp.exp(-jnp.abs(dt_in)))
    adt_t = dt_t * per_chunk(-jnp.exp(alog_ref[...]))
    a_cs_t = sum(jnp.dot(part, triu_b, preferred_element_type=F32) for part in _split_bf16(adt_t, 3))
    w_t = dt_t * jnp.exp(a_cs_t[:, CHUNK - 1:CHUNK] - a_cs_t)
    src_t = a_cs_t - jnp.log(dt_t)
    for g, r in enumerate(chunk_rows):
        heads = slice(g * SSM_HEADS, (g + 1) * SSM_HEADS)
        src_t_ref[g] = src_t[heads, :]
        acs_ref[r, :] = jnp.concatenate([a_cs_t[heads, :], pad_rows], axis=0).T
        w_ref[r, :] = jnp.concatenate([w_t[heads, :], pad_rows], axis=0).T


def _ssd_decay(dt3, dtb, alog):
    bsz, seq, _ = dt3.shape
    n_chunks = seq // CHUNK
    rows = DECAY_CHUNKS * CHUNK
    full = lambda shape: pl.BlockSpec(shape, lambda b, i: (0,) * len(shape))
    pos_major = pl.BlockSpec((None, rows, LANES), lambda b, i: (b, i, 0))
    return pl.pallas_call(
        _ssd_decay_body,
        grid=(bsz, n_chunks // DECAY_CHUNKS),
        in_specs=[pos_major, full((SSM_HEADS, 1)), full((SSM_HEADS, 1))],
        out_specs=[
            pl.BlockSpec((None, DECAY_CHUNKS, SSM_HEADS, CHUNK), lambda b, i: (b, i, 0, 0)),
            pos_major, pos_major,
        ],
        out_shape=[
            jax.ShapeDtypeStruct((bsz, n_chunks, SSM_HEADS, CHUNK), F32),
            jax.ShapeDtypeStruct((bsz, seq, LANES), F32),
            jax.ShapeDtypeStruct((bsz, seq, LANES), F32),
        ],
        compiler_params=pltpu.CompilerParams(
            dimension_semantics=("parallel", "parallel"), vmem_limit_bytes=VMEM_LIMIT),
        name="ssd_decay",
    )(dt3, dtb, alog)


def _ssd_body(xs_ref, bc_ref, src_t_ref, acs_ref, wp_ref, sel_ref, cw_ref, cb_ref, dskip_ref, o_ref,
              upad, tail, state):
    @pl.when(pl.program_id(1) == 0)
    def _():
        upad[0:CHUNK, :] = jnp.zeros((CHUNK, D_XBC), BF16)
        tail[...] = jnp.zeros_like(tail)
        state[...] = jnp.zeros_like(state)

    upad[CHUNK - CONV_TAIL:CHUNK, :] = tail[...]
    upad[CHUNK:2 * CHUNK, 0:D_SSM] = xs_ref[...]
    upad[CHUNK:2 * CHUNK, D_SSM:D_XBC] = bc_ref[...]
    shifted = jnp.dot(sel_ref[...], upad[...], preferred_element_type=F32)
    conv = cb_ref[...] + cw_ref[CONV_WIDTH - 1:CONV_WIDTH, :] * upad[CHUNK:2 * CHUNK, :].astype(F32)
    for tap in range(CONV_WIDTH - 1):
        conv = conv + cw_ref[tap:tap + 1, :] * shifted[tap * CHUNK:(tap + 1) * CHUNK, :]
    xbc = _silu(conv)
    tail[:, 0:D_SSM] = xs_ref[CHUNK - CONV_TAIL:CHUNK, :]
    tail[:, D_SSM:D_XBC] = bc_ref[CHUNK - CONV_TAIL:CHUNK, :]

    xs = xbc[:, 0:D_SSM]
    xs_b = xs.astype(BF16)
    first_head = _lane_is_first_head()
    a_cs = acs_ref[...]
    w_p = wp_ref[...]
    src_t = src_t_ref[...]
    ri = lax.broadcasted_iota(jnp.int32, (CHUNK, CHUNK), 0)
    ci = lax.broadcasted_iota(jnp.int32, (CHUNK, CHUNK), 1)
    tril = ri >= ci

    def spread(v, pairs):
        return jnp.concatenate(
            [jnp.where(first_head, v[:, 2 * p:2 * p + 1], v[:, 2 * p + 1:2 * p + 2]) for p in pairs], axis=1)

    pairs_per_group = HEAD_PAIRS // SSM_GROUPS
    for g in range(SSM_GROUPS):
        pairs = range(g * pairs_per_group, (g + 1) * pairs_per_group)
        gcols = slice(pairs[0] * LANES, (pairs[-1] + 1) * LANES)
        bg = xbc[:, D_SSM + g * SSM_STATE:D_SSM + (g + 1) * SSM_STATE]
        cg = xbc[:, D_SSM + (SSM_GROUPS + g) * SSM_STATE:D_SSM + (SSM_GROUPS + g + 1) * SSM_STATE].astype(BF16)
        cb = lax.dot_general(cg, bg.astype(BF16), (((1,), (1,)), ((), ())), preferred_element_type=F32)
        prev = state[:, gcols]
        y_off = jnp.dot(cg, prev.astype(BF16), preferred_element_type=F32)
        for i, p in enumerate(pairs):
            cols = slice(p * LANES, (p + 1) * LANES)
            xs_p = xs_b[:, cols]
            zero = jnp.zeros_like(xs_p)
            m_pair = []
            for h in (2 * p, 2 * p + 1):
                seg = a_cs[:, h:h + 1] - src_t[h:h + 1, :]
                m_pair.append((cb * jnp.exp(jnp.where(tril, seg, NEG))).astype(BF16))
            rhs = jnp.concatenate([jnp.where(first_head, xs_p, zero), jnp.where(first_head, zero, xs_p)], axis=0)
            y = jnp.dot(jnp.concatenate(m_pair, axis=1), rhs, preferred_element_type=F32)
            y = y + y_off[:, i * LANES:(i + 1) * LANES] * jnp.exp(spread(a_cs, [p])) + dskip_ref[:, cols] * xs[:, cols]
            o_ref[:, cols] = y.astype(BF16)
        xw = (xs[:, gcols] * spread(w_p, pairs)).astype(BF16)
        new = jnp.dot(bg.T.astype(BF16), xw, preferred_element_type=F32)
        state[:, gcols] = prev * jnp.exp(spread(a_cs[CHUNK - 1:CHUNK, :], pairs)) + new


def _ssd(proj3, src_t, a_cs, w_p, conv_w, conv_b, dskip):
    bsz, seq, _ = proj3.shape
    sel = np.zeros(((CONV_WIDTH - 1) * CHUNK, 2 * CHUNK), np.float32)
    for tap in range(CONV_WIDTH - 1):
        sel[tap * CHUNK + np.arange(CHUNK), CHUNK - (CONV_WIDTH - 1) + tap + np.arange(CHUNK)] = 1.0
    full = lambda shape: pl.BlockSpec(shape, lambda b, c: (0,) * len(shape))
    pos_major = pl.BlockSpec((None, CHUNK, LANES), lambda b, c: (b, c, 0))
    return pl.pallas_call(
        _ssd_body,
        grid=(bsz, seq // CHUNK),
        in_specs=[
            pl.BlockSpec((None, CHUNK, D_SSM), lambda b, c: (b, c, COL_XS)),
            pl.BlockSpec((None, CHUNK, D_BC), lambda b, c: (b, c, (COL_XS + 1) * D_SSM // D_BC)),
            pl.BlockSpec((None, None, SSM_HEADS, CHUNK), lambda b, c: (b, c, 0, 0)),
            pos_major, pos_major,
            full(sel.shape), full((CONV_WIDTH, D_XBC)), full((1, D_XBC)), full((1, D_SSM)),
        ],
        out_specs=pl.BlockSpec((None, CHUNK, D_SSM), lambda b, c: (b, c, 0)),
        out_shape=jax.ShapeDtypeStruct((bsz, seq, D_SSM), BF16),
        scratch_shapes=[
            pltpu.VMEM((2 * CHUNK, D_XBC), BF16),
            pltpu.VMEM((CONV_TAIL, D_XBC), BF16),
            pltpu.VMEM((SSM_STATE, D_SSM), F32),
        ],
        compiler_params=pltpu.CompilerParams(
            dimension_semantics=("parallel", "arbitrary"), vmem_limit_bytes=VMEM_LIMIT),
        name="ssd",
    )(proj3, proj3, src_t, a_cs, w_p, jnp.asarray(sel, BF16), conv_w, conv_b, dskip)


def _out_body(alpha, ya_ref, za_ref, ys_ref, zs_ref, x_ref, w_ref, ga_ref, gs_ref, lg_ref, lb_ref, o_ref):
    def gated_rms_norm(y_ref, gate_ref, g_ref):
        y = y_ref[...].astype(F32) * gate_ref[...].astype(F32)
        return (y * lax.rsqrt(jnp.mean(y * y, axis=-1, keepdims=True) + NORM_EPS) * g_ref[...]).astype(BF16)

    mix = jnp.concatenate([gated_rms_norm(ya_ref, za_ref, ga_ref), gated_rms_norm(ys_ref, zs_ref, gs_ref)], axis=1)
    h = alpha * x_ref[...] + jnp.dot(mix, w_ref[...], preferred_element_type=F32)
    mu = jnp.mean(h, axis=-1, keepdims=True)
    hc = h - mu
    var = jnp.mean(hc * hc, axis=-1, keepdims=True)
    o_ref[...] = hc * lax.rsqrt(var + NORM_EPS) * lg_ref[...] + lb_ref[...]


def _out_proj(alpha, y_att, proj2, y_ssm, x2d, w_out, gain_att, gain_ssm, ln_g, ln_b):
    m = x2d.shape[0]
    tm = OUT_ROWS
    rows = lambda shape: pl.BlockSpec(shape, lambda i: (i, 0))
    full = lambda shape: pl.BlockSpec(shape, lambda i: (0, 0))
    return pl.pallas_call(
        functools.partial(_out_body, alpha),
        grid=(m // tm,),
        in_specs=[
            rows((tm, D_ATT)),
            pl.BlockSpec((tm, D_ATT), lambda i: (i, COL_Z_ATT)),
            rows((tm, D_SSM)),
            pl.BlockSpec((tm, D_SSM), lambda i: (i, COL_Z_SSM)),
            rows((tm, D_MODEL)),
            full((D_ATT + D_SSM, D_MODEL)),
            full((1, D_ATT)), full((1, D_SSM)), full((1, D_MODEL)), full((1, D_MODEL)),
        ],
        out_specs=rows((tm, D_MODEL)),
        out_shape=jax.ShapeDtypeStruct((m, D_MODEL), F32),
        compiler_params=pltpu.CompilerParams(
            dimension_semantics=("parallel",), vmem_limit_bytes=VMEM_LIMIT),
        name="out_proj",
    )(y_att, proj2, y_ssm, proj2, x2d, w_out, gain_att, gain_ssm, ln_g, ln_b)


def kernel(x, w_in, conv_w, conv_b, dt_bias, a_log, d_skip, att_norm_g, ssm_norm_g, w_out, ln_g, ln_b):
    bsz, seq, _ = x.shape
    depth = w_in.shape[0]
    assert seq % UNIT == 0
    alpha = (2.0 * depth) ** 0.25
    slopes = jnp.asarray(2.0 ** (-8.0 * np.arange(1, ATT_HEADS + 1) / ATT_HEADS), dtype=F32)
    for layer in range(depth):
        w_main = w_in[layer][:, :D_PROJ].astype(BF16)
        w_dt = jnp.pad(w_in[layer][:, D_PROJ:], ((0, 0), (0, LANES - SSM_HEADS))).astype(BF16)
        x2d = x.reshape(bsz * seq, D_MODEL)
        proj, dt_raw = _in_proj(x2d, w_main, w_dt)
        proj3 = proj.reshape(bsz, seq, D_PROJ)

        y_att = _attention(proj3, slopes)

        src_t, a_cs, w_p = _ssd_decay(
            dt_raw.reshape(bsz, seq, LANES),
            dt_bias[layer].astype(F32).reshape(SSM_HEADS, 1), a_log[layer].astype(F32).reshape(SSM_HEADS, 1))
        y_ssm = _ssd(
            proj3, src_t, a_cs, w_p, conv_w[layer].astype(F32), conv_b[layer].astype(F32).reshape(1, D_XBC),
            jnp.repeat(d_skip[layer].astype(F32), HEAD_DIM).reshape(1, D_SSM))

        out = _out_proj(
            alpha, y_att.reshape(bsz * seq, D_ATT), proj, y_ssm.reshape(bsz * seq, D_SSM), x2d,
            w_out[layer].astype(BF16), att_norm_g[layer].astype(F32).reshape(1, D_ATT),
            ssm_norm_g[layer].astype(F32).reshape(1, D_SSM),
            ln_g[layer].astype(F32).reshape(1, D_MODEL), ln_b[layer].astype(F32).reshape(1, D_MODEL))
        x = out.reshape(bsz, seq, D_MODEL)
    return x
```

```python
import functools

import numpy as np
import jax
import jax.numpy as jnp
from jax import lax
from jax.experimental import pallas as pl
from jax.experimental.pallas import tpu as pltpu

F32 = jnp.float32
BF16 = jnp.bfloat16

D_MODEL = 1024
D_ATT = 1024
HEAD_DIM = 64
ATT_HEADS = D_ATT // HEAD_DIM
ATT_STEPS = 128
DILATIONS = (1, 4, 16)
D_SSM = 1024
SSM_HEADS = D_SSM // HEAD_DIM
SSM_GROUPS = 2
SSM_STATE = 128
CONV_WIDTH = 4
CHUNK = 128
D_BC = 2 * SSM_GROUPS * SSM_STATE
D_PROJ = 4 * D_ATT + D_SSM + D_SSM + D_BC
D_XBC = D_SSM + D_BC
D_XBC_START = D_PROJ - D_XBC
NORM_EPS = 1e-5

LANES = 128
HEAD_PAIRS = ATT_HEADS // 2
UNIT = DILATIONS[-1] * ATT_STEPS
WIDE = DILATIONS[-1]
PADDED_GROUP = WIDE + 8
PADDED_UNIT = UNIT // WIDE * PADDED_GROUP
MIN_DENOMINATOR = 2.0 ** -100
LOG2_E = 1.4426950408889634
Q_SCALE = HEAD_DIM ** -0.5 * LOG2_E
NEG = -1e30
PROJ_ROWS, PROJ_COLS = 2048, 512
CONV_TAIL = 16
OUT_ROWS = 512
DECAY_CHUNKS = 16
VMEM_LIMIT = 56 * 1024 * 1024

COL_Z_ATT = 3
COL_Z_SSM = 4
COL_XS = 5


def _silu(v):
    h = 0.5 * v
    return h + h * jnp.tanh(h)


def _lane_is_first_head():
    return lax.broadcasted_iota(jnp.int32, (1, LANES), 1) < HEAD_DIM


def _in_proj_body(x_ref, w_ref, wdt_ref, proj_ref, dt_ref, xb_ref):
    @pl.when(pl.program_id(1) == 0)
    def _():
        xb_ref[...] = x_ref[...].astype(BF16)
        dt_ref[...] = jnp.dot(xb_ref[...], wdt_ref[...], preferred_element_type=F32)

    def project():
        return jnp.dot(xb_ref[...], w_ref[...], preferred_element_type=F32)

    j = pl.program_id(1)
    is_q = j < D_ATT // PROJ_COLS
    is_gate = jnp.logical_and(j >= COL_Z_ATT * D_ATT // PROJ_COLS, j < COL_XS * D_SSM // PROJ_COLS)

    @pl.when(is_q)
    def _():
        proj_ref[...] = (project() * Q_SCALE).astype(BF16)

    @pl.when(is_gate)
    def _():
        proj_ref[...] = _silu(project()).astype(BF16)

    @pl.when(jnp.logical_not(jnp.logical_or(is_q, is_gate)))
    def _():
        proj_ref[...] = project().astype(BF16)


def _in_proj(x2d, w_main, w_dt):
    m = x2d.shape[0]
    tm, tn = PROJ_ROWS, PROJ_COLS
    assert m % tm == 0
    return pl.pallas_call(
        _in_proj_body,
        grid=(m // tm, D_PROJ // tn),
        in_specs=[
            pl.BlockSpec((tm, D_MODEL), lambda i, j: (jnp.minimum(i + jnp.minimum(j, 1), m // tm - 1), 0)),
            pl.BlockSpec((D_MODEL, tn), lambda i, j: (0, j)),
            pl.BlockSpec((D_MODEL, LANES), lambda i, j: (0, 0)),
        ],
        out_specs=[
            pl.BlockSpec((tm, tn), lambda i, j: (i, j)),
            pl.BlockSpec((tm, LANES), lambda i, j: (i, 0)),
        ],
        out_shape=[
            jax.ShapeDtypeStruct((m, D_PROJ), BF16),
            jax.ShapeDtypeStruct((m, LANES), F32),
        ],
        scratch_shapes=[pltpu.VMEM((tm, D_MODEL), BF16)],
        compiler_params=pltpu.CompilerParams(
            dimension_semantics=("parallel", "arbitrary"), vmem_limit_bytes=VMEM_LIMIT),
        name="in_proj",
    )(x2d, w_main, w_dt)


def _attn_body(slope_ref, q_ref, k_ref, v_ref, o_ref, qf, kf, vf, qp, kp, vp, acc, m_s, l_s, bias):
    hp, b, u = pl.program_id(0), pl.program_id(1), pl.program_id(2)
    first_head = _lane_is_first_head()
    cur = (u + 1) * UNIT
    cur_padded = (u + 1) * PADDED_UNIT

    @pl.when(jnp.logical_and(b == 0, u == 0))
    def _():
        row = lax.broadcasted_iota(jnp.int32, (2 * ATT_STEPS, 2 * ATT_STEPS), 0)
        col = lax.broadcasted_iota(jnp.int32, (2 * ATT_STEPS, 2 * ATT_STEPS), 1)
        step = (row & (ATT_STEPS - 1)) + ATT_STEPS - col
        valid = (step >= 0) & (step <= ATT_STEPS)
        slope = jnp.where(row < ATT_STEPS, slope_ref[2 * hp], slope_ref[2 * hp + 1])
        for di, d in enumerate(DILATIONS):
            table = jnp.where(valid, -(slope * LOG2_E) * (step * d).astype(F32), NEG)
            bias[2 * di] = table
            bias[2 * di + 1] = jnp.where(col < ATT_STEPS, NEG, table)

    @pl.when(jnp.logical_and(hp == 0, jnp.logical_and(b == 0, u == 0)))
    def _():
        kf[0:UNIT, :] = jnp.zeros((UNIT, LANES), F32)
        vf[0:UNIT, :] = jnp.zeros((UNIT, LANES), F32)
        kp[0:PADDED_UNIT, :] = jnp.zeros((PADDED_UNIT, LANES), F32)
        vp[0:PADDED_UNIT, :] = jnp.zeros((PADDED_UNIT, LANES), F32)

    q32, k32, v32 = q_ref[...].astype(F32), k_ref[...].astype(F32), v_ref[...].astype(F32)
    qf[...] = q32
    kf[pl.ds(cur, UNIT), :] = k32
    vf[pl.ds(cur, UNIT), :] = v32
    for g in range(UNIT // WIDE):
        group = slice(g * WIDE, (g + 1) * WIDE)
        qp[g * PADDED_GROUP:g * PADDED_GROUP + WIDE, :] = q32[group, :]
        kp[pl.ds(cur_padded + g * PADDED_GROUP, WIDE), :] = k32[group, :]
        vp[pl.ds(cur_padded + g * PADDED_GROUP, WIDE), :] = v32[group, :]

    def rows(d, start, n):
        return pl.ds(start, n) if d == 1 else pl.ds(start, n, stride=d)

    def stack_heads(q, zero):
        return jnp.concatenate([jnp.where(first_head, q, zero), jnp.where(first_head, zero, q)], axis=0)

    def load_block(d, qs):
        if d == 1:
            def reaching_back(x_ref, xf):
                before = xf[pl.ds(cur - ATT_STEPS, ATT_STEPS), :].astype(BF16)
                return jnp.concatenate([before, x_ref[0:ATT_STEPS, :]], axis=0)

            if isinstance(qs, int):
                q = q_ref[qs:qs + ATT_STEPS, :]
                if qs == 0:
                    kk, vv = reaching_back(k_ref, kf), reaching_back(v_ref, vf)
                else:
                    kk, vv = (x_ref[qs - ATT_STEPS:qs + ATT_STEPS, :] for x_ref in (k_ref, v_ref))
            else:
                q = q_ref[pl.ds(pl.multiple_of(qs, ATT_STEPS), ATT_STEPS), :]
                inside = pl.ds(pl.multiple_of(jnp.maximum(qs - ATT_STEPS, 0), ATT_STEPS), 2 * ATT_STEPS)
                kk = jnp.where(qs == 0, reaching_back(k_ref, kf), k_ref[inside, :])
                vv = jnp.where(qs == 0, reaching_back(v_ref, vf), v_ref[inside, :])
            return stack_heads(q, jnp.zeros_like(q)), kk, vv
        if d == WIDE:
            window = pl.ds(cur_padded - PADDED_UNIT + qs, 2 * ATT_STEPS, stride=PADDED_GROUP)
            q2 = stack_heads(qp[pl.ds(qs, ATT_STEPS, stride=PADDED_GROUP), :], 0.0).astype(BF16)
            return q2, kp[window, :].astype(BF16), vp[window, :].astype(BF16)
        window = rows(d, cur + qs - ATT_STEPS * d, 2 * ATT_STEPS)
        q2 = stack_heads(qf[rows(d, qs, ATT_STEPS), :], 0.0).astype(BF16)
        return q2, kf[window, :].astype(BF16), vf[window, :].astype(BF16)

    def block_scores(di, d, qs, first):
        q2, kk, vv = load_block(d, qs)
        s = lax.dot_general(q2, kk, (((1,), (1,)), ((), ())), preferred_element_type=F32)
        return s + bias[2 * di + first.astype(jnp.int32)], vv

    def weighted_values(p, vv):
        vext = jnp.concatenate([vv, jnp.ones_like(vv)], axis=1)
        oe = jnp.dot(p, vext, preferred_element_type=F32)
        o_blk = jnp.where(first_head, oe[:ATT_STEPS, :LANES], oe[ATT_STEPS:, :LANES])
        l_blk = jnp.where(first_head, oe[:ATT_STEPS, LANES:], oe[ATT_STEPS:, LANES:])
        return o_blk, l_blk

    def block_start(d, t):
        blk = t // d
        r = t - blk * d
        return blk * (ATT_STEPS * d) + r, jnp.logical_and(u == 0, blk == 0)

    n_sub = UNIT // ATT_STEPS
    order = tuple(reversed(range(len(DILATIONS))))

    for di in order:
        d = DILATIONS[di]
        for t in range(n_sub):
            qs, first = block_start(d, t)
            s, vv = block_scores(di, d, qs, first)
            o_blk, l_blk = weighted_values(jnp.exp2(s).astype(BF16), vv)
            rows_q = rows(d, qs, ATT_STEPS)
            if di == order[0]:
                acc[rows_q, :] = o_blk
                l_s[rows_q, :] = l_blk
            else:
                acc[rows_q, :] += o_blk
                l_s[rows_q, :] += l_blk

    den = l_s[...]
    in_range = (den >= MIN_DENOMINATOR) & (den < jnp.inf) & (jnp.abs(acc[...]) < jnp.inf)
    n_bad = jnp.max(jnp.max(jnp.where(in_range, 0.0, 1.0), axis=1, keepdims=True), axis=0, keepdims=True)

    @pl.when(n_bad[0, 0] > 0.0)
    def _():
        for di in order:
            d = DILATIONS[di]

            def body(t, carry, di=di, d=d):
                qs, first = block_start(d, t)
                s, vv = block_scores(di, d, qs, first)
                mb = jnp.max(s, axis=1, keepdims=True)
                o_blk, l_blk = weighted_values(jnp.exp2(s - mb).astype(BF16), vv)
                m_blk = jnp.where(first_head, mb[:ATT_STEPS], mb[ATT_STEPS:])
                rows_q = rows(d, qs, ATT_STEPS)
                if di == order[0]:
                    acc[rows_q, :] = o_blk
                    l_s[rows_q, :] = l_blk
                    m_s[rows_q, :] = m_blk
                else:
                    m_old = m_s[rows_q, :]
                    m_new = jnp.maximum(m_old, m_blk)
                    c_old = jnp.exp2(m_old - m_new)
                    c_blk = jnp.exp2(m_blk - m_new)
                    acc[rows_q, :] = acc[rows_q, :] * c_old + o_blk * c_blk
                    l_s[rows_q, :] = l_s[rows_q, :] * c_old + l_blk * c_blk
                    m_s[rows_q, :] = m_new
                return carry
            lax.fori_loop(0, n_sub, body, 0)

    o_ref[...] = (acc[...] / l_s[...]).astype(BF16)


def _attention(proj3, slopes):
    bsz, seq, _ = proj3.shape
    n_units = seq // UNIT
    blk = (None, UNIT, LANES)
    section = lambda k: pl.BlockSpec(blk, lambda h, b, u: (b, u, k * HEAD_PAIRS + h))
    return pl.pallas_call(
        _attn_body,
        grid=(HEAD_PAIRS, bsz, n_units),
        in_specs=[pl.BlockSpec(memory_space=pltpu.SMEM), section(0), section(1), section(2)],
        out_specs=section(0),
        out_shape=jax.ShapeDtypeStruct((bsz, seq, D_ATT), BF16),
        scratch_shapes=[
            pltpu.VMEM((UNIT, LANES), F32),
            pltpu.VMEM(((n_units + 1) * UNIT, LANES), F32),
            pltpu.VMEM(((n_units + 1) * UNIT, LANES), F32),
            pltpu.VMEM((PADDED_UNIT, LANES), F32),
            pltpu.VMEM(((n_units + 1) * PADDED_UNIT, LANES), F32),
            pltpu.VMEM(((n_units + 1) * PADDED_UNIT, LANES), F32),
            pltpu.VMEM((UNIT, LANES), F32),
            pltpu.VMEM((UNIT, LANES), F32),
            pltpu.VMEM((UNIT, LANES), F32),
            pltpu.VMEM((2 * len(DILATIONS), 2 * ATT_STEPS, 2 * ATT_STEPS), F32),
        ],
        compiler_params=pltpu.CompilerParams(
            dimension_semantics=("arbitrary", "arbitrary", "arbitrary"), vmem_limit_bytes=VMEM_LIMIT),
        name="dilated_attention",
    )(slopes, proj3, proj3, proj3)


def _split_bf16(v, parts):
    out = []
    for _ in range(parts):
        hi = v.astype(BF16)
        out.append(hi)
        v = v - hi.astype(F32)
    return out


def _ssd_decay_body(dt_ref, dtb_ref, alog_ref, src_t_ref, acs_ref, w_ref):
    ri = lax.broadcasted_iota(jnp.int32, (CHUNK, CHUNK), 0)
    ci = lax.broadcasted_iota(jnp.int32, (CHUNK, CHUNK), 1)
    triu_b = jnp.where(ri <= ci, 1.0, 0.0).astype(BF16)
    pad_rows = jnp.zeros((CHUNK - SSM_HEADS, CHUNK), F32)
    chunk_rows = [slice(g * CHUNK, (g + 1) * CHUNK) for g in range(DECAY_CHUNKS)]
    per_chunk = lambda v: jnp.concatenate([v] * DECAY_CHUNKS, axis=0)
    dt_in = jnp.concatenate([dt_ref[r, :].T[0:SSM_HEADS, :] for r in chunk_rows], axis=0) + per_chunk(dtb_ref[...])
    dt_t = jnp.maximum(dt_in, 0.0) + jnp.log1p(jnp.exp(-jnp.abs(dt_in)))
    adt_t = dt_t * per_chunk(-jnp.exp(alog_ref[...]))
    a_cs_t = sum(jnp.dot(part, triu_b, preferred_element_type=F32) for part in _split_bf16(adt_t, 3))
    w_t = dt_t * jnp.exp(a_cs_t[:, CHUNK - 1:CHUNK] - a_cs_t)
    src_t = a_cs_t - jnp.log(dt_t)
    for g, r in enumerate(chunk_rows):
        heads = slice(g * SSM_HEADS, (g + 1) * SSM_HEADS)
        src_t_ref[g] = src_t[heads, :]
        acs_ref[r, :] = jnp.concatenate([a_cs_t[heads, :], pad_rows], axis=0).T
        w_ref[r, :] = jnp.concatenate([w_t[heads, :], pad_rows], axis=0).T


def _ssd_decay(dt3, dtb, alog):
    bsz, seq, _ = dt3.shape
    n_chunks = seq // CHUNK
    rows = DECAY_CHUNKS * CHUNK
    full = lambda shape: pl.BlockSpec(shape, lambda b, i: (0,) * len(shape))
    pos_major = pl.BlockSpec((None, rows, LANES), lambda b, i: (b, i, 0))
    return pl.pallas_call(
        _ssd_decay_body,
        grid=(bsz, n_chunks // DECAY_CHUNKS),
        in_specs=[pos_major, full((SSM_HEADS, 1)), full((SSM_HEADS, 1))],
        out_specs=[
            pl.BlockSpec((None, DECAY_CHUNKS, SSM_HEADS, CHUNK), lambda b, i: (b, i, 0, 0)),
            pos_major, pos_major,
        ],
        out_shape=[
            jax.ShapeDtypeStruct((bsz, n_chunks, SSM_HEADS, CHUNK), F32),
            jax.ShapeDtypeStruct((bsz, seq, LANES), F32),
            jax.ShapeDtypeStruct((bsz, seq, LANES), F32),
        ],
        compiler_params=pltpu.CompilerParams(
            dimension_semantics=("parallel", "parallel"), vmem_limit_bytes=VMEM_LIMIT),
        name="ssd_decay",
    )(dt3, dtb, alog)


def _ssd_body(xs_ref, bc_ref, src_t_ref, acs_ref, wp_ref, sel_ref, cw_ref, cb_ref, dskip_ref, o_ref,
              upad, tail, state):
    @pl.when(pl.program_id(1) == 0)
    def _():
        upad[0:CHUNK, :] = jnp.zeros((CHUNK, D_XBC), BF16)
        tail[...] = jnp.zeros_like(tail)
        state[...] = jnp.zeros_like(state)

    upad[CHUNK - CONV_TAIL:CHUNK, :] = tail[...]
    upad[CHUNK:2 * CHUNK, 0:D_SSM] = xs_ref[...]
    upad[CHUNK:2 * CHUNK, D_SSM:D_XBC] = bc_ref[...]

    def conv_silu(c0, c1):
        shifted = jnp.dot(sel_ref[...], upad[:, c0:c1], preferred_element_type=F32)
        conv = cb_ref[:, c0:c1] + cw_ref[CONV_WIDTH - 1:CONV_WIDTH, c0:c1] * upad[CHUNK:2 * CHUNK, c0:c1].astype(F32)
        for tap in range(CONV_WIDTH - 1):
            conv = conv + cw_ref[tap:tap + 1, c0:c1] * shifted[tap * CHUNK:(tap + 1) * CHUNK, :]
        return _silu(conv)

    bc = conv_silu(D_SSM, D_XBC)
    tail[:, 0:D_SSM] = xs_ref[CHUNK - CONV_TAIL:CHUNK, :]
    tail[:, D_SSM:D_XBC] = bc_ref[CHUNK - CONV_TAIL:CHUNK, :]

    first_head = _lane_is_first_head()
    a_cs = acs_ref[...]
    w_p = wp_ref[...]
    src_t = src_t_ref[...]
    ri = lax.broadcasted_iota(jnp.int32, (CHUNK, CHUNK), 0)
    ci = lax.broadcasted_iota(jnp.int32, (CHUNK, CHUNK), 1)
    tril = ri >= ci

    def spread(v, pairs):
        return jnp.concatenate(
            [jnp.where(first_head, v[:, 2 * p:2 * p + 1], v[:, 2 * p + 1:2 * p + 2]) for p in pairs], axis=1)

    pairs_per_group = HEAD_PAIRS // SSM_GROUPS
    for g in range(SSM_GROUPS):
        pairs = range(g * pairs_per_group, (g + 1) * pairs_per_group)
        gcols = slice(pairs[0] * LANES, (pairs[-1] + 1) * LANES)
        xs = conv_silu(gcols.start, gcols.stop)
        xs_b = xs.astype(BF16)
        bg = bc[:, g * SSM_STATE:(g + 1) * SSM_STATE]
        cg = bc[:, (SSM_GROUPS + g) * SSM_STATE:(SSM_GROUPS + g + 1) * SSM_STATE].astype(BF16)
        cb = lax.dot_general(cg, bg.astype(BF16), (((1,), (1,)), ((), ())), preferred_element_type=F32)
        prev = state[:, gcols]
        y_off = jnp.dot(cg, prev.astype(BF16), preferred_element_type=F32)
        for i, p in enumerate(pairs):
            cols = slice(p * LANES, (p + 1) * LANES)
            local = slice(i * LANES, (i + 1) * LANES)
            xs_p = xs_b[:, local]
            zero = jnp.zeros_like(xs_p)
            m_pair = []
            for h in (2 * p, 2 * p + 1):
                seg = a_cs[:, h:h + 1] - src_t[h:h + 1, :]
                m_pair.append((cb * jnp.exp(jnp.where(tril, seg, NEG))).astype(BF16))
            rhs = jnp.concatenate([jnp.where(first_head, xs_p, zero), jnp.where(first_head, zero, xs_p)], axis=0)
            y = jnp.dot(jnp.concatenate(m_pair, axis=1), rhs, preferred_element_type=F32)
            y = y + y_off[:, local] * jnp.exp(spread(a_cs, [p])) + dskip_ref[:, cols] * xs[:, local]
            o_ref[:, cols] = y.astype(BF16)
        xw = (xs * spread(w_p, pairs)).astype(BF16)
        new = jnp.dot(bg.T.astype(BF16), xw, preferred_element_type=F32)
        state[:, gcols] = prev * jnp.exp(spread(a_cs[CHUNK - 1:CHUNK, :], pairs)) + new


def _ssd(proj3, src_t, a_cs, w_p, conv_w, conv_b, dskip):
    bsz, seq, _ = proj3.shape
    sel = np.zeros(((CONV_WIDTH - 1) * CHUNK, 2 * CHUNK), np.float32)
    for tap in range(CONV_WIDTH - 1):
        sel[tap * CHUNK + np.arange(CHUNK), CHUNK - (CONV_WIDTH - 1) + tap + np.arange(CHUNK)] = 1.0
    full = lambda shape: pl.BlockSpec(shape, lambda b, c: (0,) * len(shape))
    pos_major = pl.BlockSpec((None, CHUNK, LANES), lambda b, c: (b, c, 0))
    return pl.pallas_call(
        _ssd_body,
        grid=(bsz, seq // CHUNK),
        in_specs=[
            pl.BlockSpec((None, CHUNK, D_SSM), lambda b, c: (b, c, COL_XS)),
            pl.BlockSpec((None, CHUNK, D_BC), lambda b, c: (b, c, (COL_XS + 1) * D_SSM // D_BC)),
            pl.BlockSpec((None, None, SSM_HEADS, CHUNK), lambda b, c: (b, c, 0, 0)),
            pos_major, pos_major,
            full(sel.shape), full((CONV_WIDTH, D_XBC)), full((1, D_XBC)), full((1, D_SSM)),
        ],
        out_specs=pl.BlockSpec((None, CHUNK, D_SSM), lambda b, c: (b, c, 0)),
        out_shape=jax.ShapeDtypeStruct((bsz, seq, D_SSM), BF16),
        scratch_shapes=[
            pltpu.VMEM((2 * CHUNK, D_XBC), BF16),
            pltpu.VMEM((CONV_TAIL, D_XBC), BF16),
            pltpu.VMEM((SSM_STATE, D_SSM), F32),
        ],
        compiler_params=pltpu.CompilerParams(
            dimension_semantics=("parallel", "arbitrary"), vmem_limit_bytes=VMEM_LIMIT),
        name="ssd",
    )(proj3, proj3, src_t, a_cs, w_p, jnp.asarray(sel, BF16), conv_w, conv_b, dskip)


def _out_body(alpha, ya_ref, za_ref, ys_ref, zs_ref, x_ref, w_ref, ga_ref, gs_ref, lg_ref, lb_ref, o_ref):
    def gated_rms_norm(y_ref, gate_ref, g_ref):
        y = y_ref[...].astype(F32) * gate_ref[...].astype(F32)
        return (y * lax.rsqrt(jnp.mean(y * y, axis=-1, keepdims=True) + NORM_EPS) * g_ref[...]).astype(BF16)

    mix = jnp.concatenate([gated_rms_norm(ya_ref, za_ref, ga_ref), gated_rms_norm(ys_ref, zs_ref, gs_ref)], axis=1)
    h = alpha * x_ref[...] + jnp.dot(mix, w_ref[...], preferred_element_type=F32)
    mu = jnp.mean(h, axis=-1, keepdims=True)
    hc = h - mu
    var = jnp.mean(hc * hc, axis=-1, keepdims=True)
    o_ref[...] = hc * lax.rsqrt(var + NORM_EPS) * lg_ref[...] + lb_ref[...]


def _out_proj(alpha, y_att, proj2, y_ssm, x2d, w_out, gain_att, gain_ssm, ln_g, ln_b):
    m = x2d.shape[0]
    tm = OUT_ROWS
    rows = lambda shape: pl.BlockSpec(shape, lambda i: (i, 0))
    full = lambda shape: pl.BlockSpec(shape, lambda i: (0, 0))
    return pl.pallas_call(
        functools.partial(_out_body, alpha),
        grid=(m // tm,),
        in_specs=[
            rows((tm, D_ATT)),
            pl.BlockSpec((tm, D_ATT), lambda i: (i, COL_Z_ATT)),
            rows((tm, D_SSM)),
            pl.BlockSpec((tm, D_SSM), lambda i: (i, COL_Z_SSM)),
            rows((tm, D_MODEL)),
            full((D_ATT + D_SSM, D_MODEL)),
            full((1, D_ATT)), full((1, D_SSM)), full((1, D_MODEL)), full((1, D_MODEL)),
        ],
        out_specs=rows((tm, D_MODEL)),
        out_shape=jax.ShapeDtypeStruct((m, D_MODEL), F32),
        compiler_params=pltpu.CompilerParams(
            dimension_semantics=("parallel",), vmem_limit_bytes=VMEM_LIMIT),
        name="out_proj",
    )(y_att, proj2, y_ssm, proj2, x2d, w_out, gain_att, gain_ssm, ln_g, ln_b)


def kernel(x, w_in, conv_w, conv_b, dt_bias, a_log, d_skip, att_norm_g, ssm_norm_g, w_out, ln_g, ln_b):
    bsz, seq, _ = x.shape
    depth = w_in.shape[0]
    assert seq % UNIT == 0
    alpha = (2.0 * depth) ** 0.25
    slopes = jnp.asarray(2.0 ** (-8.0 * np.arange(1, ATT_HEADS + 1) / ATT_HEADS), dtype=F32)
    for layer in range(depth):
        w_main = w_in[layer][:, :D_PROJ].astype(BF16)
        w_dt = jnp.pad(w_in[layer][:, D_PROJ:], ((0, 0), (0, LANES - SSM_HEADS))).astype(BF16)
        x2d = x.reshape(bsz * seq, D_MODEL)
        proj, dt_raw = _in_proj(x2d, w_main, w_dt)
        proj3 = proj.reshape(bsz, seq, D_PROJ)

        y_att = _attention(proj3, slopes)

        src_t, a_cs, w_p = _ssd_decay(
            dt_raw.reshape(bsz, seq, LANES),
            dt_bias[layer].astype(F32).reshape(SSM_HEADS, 1), a_log[layer].astype(F32).reshape(SSM_HEADS, 1))
        y_ssm = _ssd(
            proj3, src_t, a_cs, w_p, conv_w[layer].astype(F32), conv_b[layer].astype(F32).reshape(1, D_XBC),
            jnp.repeat(d_skip[layer].astype(F32), HEAD_DIM).reshape(1, D_SSM))

        out = _out_proj(
            alpha, y_att.reshape(bsz * seq, D_ATT), proj, y_ssm.reshape(bsz * seq, D_SSM), x2d,
            w_out[layer].astype(BF16), att_norm_g[layer].astype(F32).reshape(1, D_ATT),
            ssm_norm_g[layer].astype(F32).reshape(1, D_SSM),
            ln_g[layer].astype(F32).reshape(1, D_MODEL), ln_b[layer].astype(F32).reshape(1, D_MODEL))
        x = out.reshape(bsz, seq, D_MODEL)
    return x
```

```python
import functools

import numpy as np
import jax
import jax.numpy as jnp
from jax import lax
from jax.experimental import pallas as pl
from jax.experimental.pallas import tpu as pltpu

F32 = jnp.float32
BF16 = jnp.bfloat16

D_MODEL = 1024
D_ATT = 1024
HEAD_DIM = 64
ATT_HEADS = D_ATT // HEAD_DIM
ATT_STEPS = 128
DILATIONS = (1, 4, 16)
D_SSM = 1024
SSM_HEADS = D_SSM // HEAD_DIM
SSM_GROUPS = 2
SSM_STATE = 128
CONV_WIDTH = 4
CHUNK = 128
D_BC = 2 * SSM_GROUPS * SSM_STATE
D_PROJ = 4 * D_ATT + D_SSM + D_SSM + D_BC
D_XBC = D_SSM + D_BC
D_XBC_START = D_PROJ - D_XBC
NORM_EPS = 1e-5

LANES = 128
HEAD_PAIRS = ATT_HEADS // 2
UNIT = DILATIONS[-1] * ATT_STEPS
WIDE = DILATIONS[-1]
PADDED_GROUP = WIDE + 4
PADDED_UNIT = UNIT // WIDE * PADDED_GROUP
MIN_DENOMINATOR = 2.0 ** -100
LOG2_E = 1.4426950408889634
Q_SCALE = HEAD_DIM ** -0.5 * LOG2_E
NEG = -1e30
PROJ_ROWS, PROJ_COLS = 2048, 512
CONV_TAIL = 16
OUT_ROWS = 512
DECAY_CHUNKS = 16
VMEM_LIMIT = 56 * 1024 * 1024

COL_Z_ATT = 3
COL_Z_SSM = 4
COL_XS = 5


def _silu(v):
    h = 0.5 * v
    return h + h * jnp.tanh(h)


def _lane_is_first_head():
    return lax.broadcasted_iota(jnp.int32, (1, LANES), 1) < HEAD_DIM


def _in_proj_body(x_ref, w_ref, wdt_ref, proj_ref, dt_ref, xb_ref):
    @pl.when(pl.program_id(1) == 0)
    def _():
        xb_ref[...] = x_ref[...].astype(BF16)
        dt_ref[...] = jnp.dot(xb_ref[...], wdt_ref[...], preferred_element_type=F32)

    def project():
        return jnp.dot(xb_ref[...], w_ref[...], preferred_element_type=F32)

    j = pl.program_id(1)
    is_q = j < D_ATT // PROJ_COLS
    is_gate = jnp.logical_and(j >= COL_Z_ATT * D_ATT // PROJ_COLS, j < COL_XS * D_SSM // PROJ_COLS)

    @pl.when(is_q)
    def _():
        proj_ref[...] = (project() * Q_SCALE).astype(BF16)

    @pl.when(is_gate)
    def _():
        proj_ref[...] = _silu(project()).astype(BF16)

    @pl.when(jnp.logical_not(jnp.logical_or(is_q, is_gate)))
    def _():
        proj_ref[...] = project().astype(BF16)


def _in_proj(x2d, w_main, w_dt):
    m = x2d.shape[0]
    tm, tn = PROJ_ROWS, PROJ_COLS
    assert m % tm == 0
    return pl.pallas_call(
        _in_proj_body,
        grid=(m // tm, D_PROJ // tn),
        in_specs=[
            pl.BlockSpec((tm, D_MODEL), lambda i, j: (jnp.minimum(i + jnp.minimum(j, 1), m // tm - 1), 0)),
            pl.BlockSpec((D_MODEL, tn), lambda i, j: (0, j)),
            pl.BlockSpec((D_MODEL, LANES), lambda i, j: (0, 0)),
        ],
        out_specs=[
            pl.BlockSpec((tm, tn), lambda i, j: (i, j)),
            pl.BlockSpec((tm, LANES), lambda i, j: (i, 0)),
        ],
        out_shape=[
            jax.ShapeDtypeStruct((m, D_PROJ), BF16),
            jax.ShapeDtypeStruct((m, LANES), F32),
        ],
        scratch_shapes=[pltpu.VMEM((tm, D_MODEL), BF16)],
        compiler_params=pltpu.CompilerParams(
            dimension_semantics=("parallel", "arbitrary"), vmem_limit_bytes=VMEM_LIMIT),
        name="in_proj",
    )(x2d, w_main, w_dt)


def _attn_body(slope_ref, q_ref, k_ref, v_ref, o_ref, qf, kf, vf, qp, kp, vp, acc, m_s, l_s, bias):
    hp, b, u = pl.program_id(0), pl.program_id(1), pl.program_id(2)
    first_head = _lane_is_first_head()
    cur = (u + 1) * UNIT
    cur_padded = (u + 1) * PADDED_UNIT

    @pl.when(jnp.logical_and(b == 0, u == 0))
    def _():
        row = lax.broadcasted_iota(jnp.int32, (2 * ATT_STEPS, 2 * ATT_STEPS), 0)
        col = lax.broadcasted_iota(jnp.int32, (2 * ATT_STEPS, 2 * ATT_STEPS), 1)
        step = (row & (ATT_STEPS - 1)) + ATT_STEPS - col
        valid = (step >= 0) & (step <= ATT_STEPS)
        slope = jnp.where(row < ATT_STEPS, slope_ref[2 * hp], slope_ref[2 * hp + 1])
        for di, d in enumerate(DILATIONS):
            table = jnp.where(valid, -(slope * LOG2_E) * (step * d).astype(F32), NEG)
            bias[2 * di] = table
            bias[2 * di + 1] = jnp.where(col < ATT_STEPS, NEG, table)

    @pl.when(jnp.logical_and(hp == 0, jnp.logical_and(b == 0, u == 0)))
    def _():
        kf[0:UNIT, :] = jnp.zeros((UNIT, LANES), F32)
        vf[0:UNIT, :] = jnp.zeros((UNIT, LANES), F32)
        kp[0:PADDED_UNIT, :] = jnp.zeros((PADDED_UNIT, LANES), F32)
        vp[0:PADDED_UNIT, :] = jnp.zeros((PADDED_UNIT, LANES), F32)

    q32, k32, v32 = q_ref[...].astype(F32), k_ref[...].astype(F32), v_ref[...].astype(F32)
    qf[...] = q32
    kf[pl.ds(cur, UNIT), :] = k32
    vf[pl.ds(cur, UNIT), :] = v32
    for g in range(UNIT // WIDE):
        group = slice(g * WIDE, (g + 1) * WIDE)
        qp[g * PADDED_GROUP:g * PADDED_GROUP + WIDE, :] = q32[group, :]
        kp[pl.ds(cur_padded + g * PADDED_GROUP, WIDE), :] = k32[group, :]
        vp[pl.ds(cur_padded + g * PADDED_GROUP, WIDE), :] = v32[group, :]

    def rows(d, start, n):
        return pl.ds(start, n) if d == 1 else pl.ds(start, n, stride=d)

    def stack_heads(q, zero):
        return jnp.concatenate([jnp.where(first_head, q, zero), jnp.where(first_head, zero, q)], axis=0)

    def load_block(d, qs):
        if d == 1:
            def reaching_back(x_ref, xf):
                before = xf[pl.ds(cur - ATT_STEPS, ATT_STEPS), :].astype(BF16)
                return jnp.concatenate([before, x_ref[0:ATT_STEPS, :]], axis=0)

            if isinstance(qs, int):
                q = q_ref[qs:qs + ATT_STEPS, :]
                if qs == 0:
                    kk, vv = reaching_back(k_ref, kf), reaching_back(v_ref, vf)
                else:
                    kk, vv = (x_ref[qs - ATT_STEPS:qs + ATT_STEPS, :] for x_ref in (k_ref, v_ref))
            else:
                q = q_ref[pl.ds(pl.multiple_of(qs, ATT_STEPS), ATT_STEPS), :]
                inside = pl.ds(pl.multiple_of(jnp.maximum(qs - ATT_STEPS, 0), ATT_STEPS), 2 * ATT_STEPS)
                kk = jnp.where(qs == 0, reaching_back(k_ref, kf), k_ref[inside, :])
                vv = jnp.where(qs == 0, reaching_back(v_ref, vf), v_ref[inside, :])
            return stack_heads(q, jnp.zeros_like(q)), kk, vv
        if d == WIDE:
            window = pl.ds(cur_padded - PADDED_UNIT + qs, 2 * ATT_STEPS, stride=PADDED_GROUP)
            q2 = stack_heads(qp[pl.ds(qs, ATT_STEPS, stride=PADDED_GROUP), :], 0.0).astype(BF16)
            return q2, kp[window, :].astype(BF16), vp[window, :].astype(BF16)
        window = rows(d, cur + qs - ATT_STEPS * d, 2 * ATT_STEPS)
        q2 = stack_heads(qf[rows(d, qs, ATT_STEPS), :], 0.0).astype(BF16)
        return q2, kf[window, :].astype(BF16), vf[window, :].astype(BF16)

    def block_scores(di, d, qs, first):
        q2, kk, vv = load_block(d, qs)
        s = lax.dot_general(q2, kk, (((1,), (1,)), ((), ())), preferred_element_type=F32)
        return s + bias[2 * di + first.astype(jnp.int32)], vv

    def weighted_values(p, vv):
        vext = jnp.concatenate([vv, jnp.ones_like(vv)], axis=1)
        oe = jnp.dot(p, vext, preferred_element_type=F32)
        o_blk = jnp.where(first_head, oe[:ATT_STEPS, :LANES], oe[ATT_STEPS:, :LANES])
        l_blk = jnp.where(first_head, oe[:ATT_STEPS, LANES:], oe[ATT_STEPS:, LANES:])
        return o_blk, l_blk

    def block_start(d, t):
        blk = t // d
        r = t - blk * d
        return blk * (ATT_STEPS * d) + r, jnp.logical_and(u == 0, blk == 0)

    n_sub = UNIT // ATT_STEPS
    order = tuple(reversed(range(len(DILATIONS))))

    for di in order:
        d = DILATIONS[di]
        for t in range(n_sub):
            qs, first = block_start(d, t)
            s, vv = block_scores(di, d, qs, first)
            o_blk, l_blk = weighted_values(jnp.exp2(s).astype(BF16), vv)
            rows_q = rows(d, qs, ATT_STEPS)
            if di == order[0]:
                acc[rows_q, :] = o_blk
                l_s[rows_q, :] = l_blk
            else:
                acc[rows_q, :] += o_blk
                l_s[rows_q, :] += l_blk

    den = l_s[...]
    in_range = (den >= MIN_DENOMINATOR) & (den < jnp.inf) & (jnp.abs(acc[...]) < jnp.inf)
    n_bad = jnp.max(jnp.max(jnp.where(in_range, 0.0, 1.0), axis=1, keepdims=True), axis=0, keepdims=True)

    @pl.when(n_bad[0, 0] > 0.0)
    def _():
        for di in order:
            d = DILATIONS[di]

            def body(t, carry, di=di, d=d):
                qs, first = block_start(d, t)
                s, vv = block_scores(di, d, qs, first)
                mb = jnp.max(s, axis=1, keepdims=True)
                o_blk, l_blk = weighted_values(jnp.exp2(s - mb).astype(BF16), vv)
                m_blk = jnp.where(first_head, mb[:ATT_STEPS], mb[ATT_STEPS:])
                rows_q = rows(d, qs, ATT_STEPS)
                if di == order[0]:
                    acc[rows_q, :] = o_blk
                    l_s[rows_q, :] = l_blk
                    m_s[rows_q, :] = m_blk
                else:
                    m_old = m_s[rows_q, :]
                    m_new = jnp.maximum(m_old, m_blk)
                    c_old = jnp.exp2(m_old - m_new)
                    c_blk = jnp.exp2(m_blk - m_new)
                    acc[rows_q, :] = acc[rows_q, :] * c_old + o_blk * c_blk
                    l_s[rows_q, :] = l_s[rows_q, :] * c_old + l_blk * c_blk
                    m_s[rows_q, :] = m_new
                return carry
            lax.fori_loop(0, n_sub, body, 0)

    o_ref[...] = (acc[...] / l_s[...]).astype(BF16)


def _attention(proj3, slopes):
    bsz, seq, _ = proj3.shape
    n_units = seq // UNIT
    blk = (None, UNIT, LANES)
    section = lambda k: pl.BlockSpec(blk, lambda h, b, u: (b, u, k * HEAD_PAIRS + h))
    return pl.pallas_call(
        _attn_body,
        grid=(HEAD_PAIRS, bsz, n_units),
        in_specs=[pl.BlockSpec(memory_space=pltpu.SMEM), section(0), section(1), section(2)],
        out_specs=section(0),
        out_shape=jax.ShapeDtypeStruct((bsz, seq, D_ATT), BF16),
        scratch_shapes=[
            pltpu.VMEM((UNIT, LANES), F32),
            pltpu.VMEM(((n_units + 1) * UNIT, LANES), F32),
            pltpu.VMEM(((n_units + 1) * UNIT, LANES), F32),
            pltpu.VMEM((PADDED_UNIT, LANES), F32),
            pltpu.VMEM(((n_units + 1) * PADDED_UNIT, LANES), F32),
            pltpu.VMEM(((n_units + 1) * PADDED_UNIT, LANES), F32),
            pltpu.VMEM((UNIT, LANES), F32),
            pltpu.VMEM((UNIT, LANES), F32),
            pltpu.VMEM((UNIT, LANES), F32),
            pltpu.VMEM((2 * len(DILATIONS), 2 * ATT_STEPS, 2 * ATT_STEPS), F32),
        ],
        compiler_params=pltpu.CompilerParams(
            dimension_semantics=("arbitrary", "arbitrary", "arbitrary"), vmem_limit_bytes=VMEM_LIMIT),
        name="dilated_attention",
    )(slopes, proj3, proj3, proj3)


def _split_bf16(v, parts):
    out = []
    for _ in range(parts):
        hi = v.astype(BF16)
        out.append(hi)
        v = v - hi.astype(F32)
    return out


def _ssd_decay_body(dt_ref, dtb_ref, alog_ref, src_t_ref, acs_ref, w_ref):
    ri = lax.broadcasted_iota(jnp.int32, (CHUNK, CHUNK), 0)
    ci = lax.broadcasted_iota(jnp.int32, (CHUNK, CHUNK), 1)
    triu_b = jnp.where(ri <= ci, 1.0, 0.0).astype(BF16)
    pad_rows = jnp.zeros((CHUNK - SSM_HEADS, CHUNK), F32)
    chunk_rows = [slice(g * CHUNK, (g + 1) * CHUNK) for g in range(DECAY_CHUNKS)]
    per_chunk = lambda v: jnp.concatenate([v] * DECAY_CHUNKS, axis=0)
    dt_in = jnp.concatenate([dt_ref[r, :].T[0:SSM_HEADS, :] for r in chunk_rows], axis=0) + per_chunk(dtb_ref[...])
    dt_t = jnp.maximum(dt_in, 0.0) + jnp.log1p(jnp.exp(-jnp.abs(dt_in)))
    adt_t = dt_t * per_chunk(-jnp.exp(alog_ref[...]))
    a_cs_t = sum(jnp.dot(part, triu_b, preferred_element_type=F32) for part in _split_bf16(adt_t, 3))
    w_t = dt_t * jnp.exp(a_cs_t[:, CHUNK - 1:CHUNK] - a_cs_t)
    src_t = a_cs_t - jnp.log(dt_t)
    for g, r in enumerate(chunk_rows):
        heads = slice(g * SSM_HEADS, (g + 1) * SSM_HEADS)
        src_t_ref[g] = src_t[heads, :]
        acs_ref[r, :] = jnp.concatenate([a_cs_t[heads, :], pad_rows], axis=0).T
        w_ref[r, :] = jnp.concatenate([w_t[heads, :], pad_rows], axis=0).T


def _ssd_decay(dt3, dtb, alog):
    bsz, seq, _ = dt3.shape
    n_chunks = seq // CHUNK
    rows = DECAY_CHUNKS * CHUNK
    full = lambda shape: pl.BlockSpec(shape, lambda b, i: (0,) * len(shape))
    pos_major = pl.BlockSpec((None, rows, LANES), lambda b, i: (b, i, 0))
    return pl.pallas_call(
        _ssd_decay_body,
        grid=(bsz, n_chunks // DECAY_CHUNKS),
        in_specs=[pos_major, full((SSM_HEADS, 1)), full((SSM_HEADS, 1))],
        out_specs=[
            pl.BlockSpec((None, DECAY_CHUNKS, SSM_HEADS, CHUNK), lambda b, i: (b, i, 0, 0)),
            pos_major, pos_major,
        ],
        out_shape=[
            jax.ShapeDtypeStruct((bsz, n_chunks, SSM_HEADS, CHUNK), F32),
            jax.ShapeDtypeStruct((bsz, seq, LANES), F32),
            jax.ShapeDtypeStruct((bsz, seq, LANES), F32),
        ],
        compiler_params=pltpu.CompilerParams(
            dimension_semantics=("parallel", "parallel"), vmem_limit_bytes=VMEM_LIMIT),
        name="ssd_decay",
    )(dt3, dtb, alog)


def _ssd_body(xs_ref, bc_ref, src_t_ref, acs_ref, wp_ref, sel_ref, cw_ref, cb_ref, dskip_ref, o_ref,
              upad, tail, state):
    @pl.when(pl.program_id(1) == 0)
    def _():
        upad[0:CHUNK, :] = jnp.zeros((CHUNK, D_XBC), BF16)
        tail[...] = jnp.zeros_like(tail)
        state[...] = jnp.zeros_like(state)

    upad[CHUNK - CONV_TAIL:CHUNK, :] = tail[...]
    upad[CHUNK:2 * CHUNK, 0:D_SSM] = xs_ref[...]
    upad[CHUNK:2 * CHUNK, D_SSM:D_XBC] = bc_ref[...]

    def conv_silu(c0, c1):
        shifted = jnp.dot(sel_ref[...], upad[:, c0:c1], preferred_element_type=F32)
        conv = cb_ref[:, c0:c1] + cw_ref[CONV_WIDTH - 1:CONV_WIDTH, c0:c1] * upad[CHUNK:2 * CHUNK, c0:c1].astype(F32)
        for tap in range(CONV_WIDTH - 1):
            conv = conv + cw_ref[tap:tap + 1, c0:c1] * shifted[tap * CHUNK:(tap + 1) * CHUNK, :]
        return _silu(conv)

    bc = conv_silu(D_SSM, D_XBC)
    tail[:, 0:D_SSM] = xs_ref[CHUNK - CONV_TAIL:CHUNK, :]
    tail[:, D_SSM:D_XBC] = bc_ref[CHUNK - CONV_TAIL:CHUNK, :]

    first_head = _lane_is_first_head()
    a_cs = acs_ref[...]
    w_p = wp_ref[...]
    src_t = src_t_ref[...]
    ri = lax.broadcasted_iota(jnp.int32, (CHUNK, CHUNK), 0)
    ci = lax.broadcasted_iota(jnp.int32, (CHUNK, CHUNK), 1)
    tril = ri >= ci

    def spread(v, pairs):
        return jnp.concatenate(
            [jnp.where(first_head, v[:, 2 * p:2 * p + 1], v[:, 2 * p + 1:2 * p + 2]) for p in pairs], axis=1)

    pairs_per_group = HEAD_PAIRS // SSM_GROUPS
    for g in range(SSM_GROUPS):
        pairs = range(g * pairs_per_group, (g + 1) * pairs_per_group)
        gcols = slice(pairs[0] * LANES, (pairs[-1] + 1) * LANES)
        xs = conv_silu(gcols.start, gcols.stop)
        xs_b = xs.astype(BF16)
        bg = bc[:, g * SSM_STATE:(g + 1) * SSM_STATE]
        cg = bc[:, (SSM_GROUPS + g) * SSM_STATE:(SSM_GROUPS + g + 1) * SSM_STATE].astype(BF16)
        cb = lax.dot_general(cg, bg.astype(BF16), (((1,), (1,)), ((), ())), preferred_element_type=F32)
        prev = state[:, gcols]
        y_off = jnp.dot(cg, prev.astype(BF16), preferred_element_type=F32)
        for i, p in enumerate(pairs):
            cols = slice(p * LANES, (p + 1) * LANES)
            local = slice(i * LANES, (i + 1) * LANES)
            xs_p = xs_b[:, local]
            zero = jnp.zeros_like(xs_p)
            m_pair = []
            for h in (2 * p, 2 * p + 1):
                seg = a_cs[:, h:h + 1] - src_t[h:h + 1, :]
                m_pair.append((cb * jnp.exp(jnp.where(tril, seg, NEG))).astype(BF16))
            rhs = jnp.concatenate([jnp.where(first_head, xs_p, zero), jnp.where(first_head, zero, xs_p)], axis=0)
            y = jnp.dot(jnp.concatenate(m_pair, axis=1), rhs, preferred_element_type=F32)
            y = y + y_off[:, local] * jnp.exp(spread(a_cs, [p])) + dskip_ref[:, cols] * xs[:, local]
            o_ref[:, cols] = y.astype(BF16)
        xw = (xs * spread(w_p, pairs)).astype(BF16)
        new = jnp.dot(bg.T.astype(BF16), xw, preferred_element_type=F32)
        state[:, gcols] = prev * jnp.exp(spread(a_cs[CHUNK - 1:CHUNK, :], pairs)) + new


def _ssd(proj3, src_t, a_cs, w_p, conv_w, conv_b, dskip):
    bsz, seq, _ = proj3.shape
    sel = np.zeros(((CONV_WIDTH - 1) * CHUNK, 2 * CHUNK), np.float32)
    for tap in range(CONV_WIDTH - 1):
        sel[tap * CHUNK + np.arange(CHUNK), CHUNK - (CONV_WIDTH - 1) + tap + np.arange(CHUNK)] = 1.0
    full = lambda shape: pl.BlockSpec(shape, lambda b, c: (0,) * len(shape))
    pos_major = pl.BlockSpec((None, CHUNK, LANES), lambda b, c: (b, c, 0))
    return pl.pallas_call(
        _ssd_body,
        grid=(bsz, seq // CHUNK),
        in_specs=[
            pl.BlockSpec((None, CHUNK, D_SSM), lambda b, c: (b, c, COL_XS)),
            pl.BlockSpec((None, CHUNK, D_BC), lambda b, c: (b, c, (COL_XS + 1) * D_SSM // D_BC)),
            pl.BlockSpec((None, None, SSM_HEADS, CHUNK), lambda b, c: (b, c, 0, 0)),
            pos_major, pos_major,
            full(sel.shape), full((CONV_WIDTH, D_XBC)), full((1, D_XBC)), full((1, D_SSM)),
        ],
        out_specs=pl.BlockSpec((None, CHUNK, D_SSM), lambda b, c: (b, c, 0)),
        out_shape=jax.ShapeDtypeStruct((bsz, seq, D_SSM), BF16),
        scratch_shapes=[
            pltpu.VMEM((2 * CHUNK, D_XBC), BF16),
            pltpu.VMEM((CONV_TAIL, D_XBC), BF16),
            pltpu.VMEM((SSM_STATE, D_SSM), F32),
        ],
        compiler_params=pltpu.CompilerParams(
            dimension_semantics=("parallel", "arbitrary"), vmem_limit_bytes=VMEM_LIMIT),
        name="ssd",
    )(proj3, proj3, src_t, a_cs, w_p, jnp.asarray(sel, BF16), conv_w, conv_b, dskip)


def _out_body(alpha, ya_ref, za_ref, ys_ref, zs_ref, x_ref, w_ref, ga_ref, gs_ref, lg_ref, lb_ref, o_ref):
    def gated_rms_norm(y_ref, gate_ref, g_ref):
        y = y_ref[...].astype(F32) * gate_ref[...].astype(F32)
        return (y * lax.rsqrt(jnp.mean(y * y, axis=-1, keepdims=True) + NORM_EPS) * g_ref[...]).astype(BF16)

    mix = jnp.concatenate([gated_rms_norm(ya_ref, za_ref, ga_ref), gated_rms_norm(ys_ref, zs_ref, gs_ref)], axis=1)
    h = alpha * x_ref[...] + jnp.dot(mix, w_ref[...], preferred_element_type=F32)
    mu = jnp.mean(h, axis=-1, keepdims=True)
    hc = h - mu
    var = jnp.mean(hc * hc, axis=-1, keepdims=True)
    o_ref[...] = hc * lax.rsqrt(var + NORM_EPS) * lg_ref[...] + lb_ref[...]


def _out_proj(alpha, y_att, proj2, y_ssm, x2d, w_out, gain_att, gain_ssm, ln_g, ln_b):
    m = x2d.shape[0]
    tm = OUT_ROWS
    rows = lambda shape: pl.BlockSpec(shape, lambda i: (i, 0))
    full = lambda shape: pl.BlockSpec(shape, lambda i: (0, 0))
    return pl.pallas_call(
        functools.partial(_out_body, alpha),
        grid=(m // tm,),
        in_specs=[
            rows((tm, D_ATT)),
            pl.BlockSpec((tm, D_ATT), lambda i: (i, COL_Z_ATT)),
            rows((tm, D_SSM)),
            pl.BlockSpec((tm, D_SSM), lambda i: (i, COL_Z_SSM)),
            rows((tm, D_MODEL)),
            full((D_ATT + D_SSM, D_MODEL)),
            full((1, D_ATT)), full((1, D_SSM)), full((1, D_MODEL)), full((1, D_MODEL)),
        ],
        out_specs=rows((tm, D_MODEL)),
        out_shape=jax.ShapeDtypeStruct((m, D_MODEL), F32),
        compiler_params=pltpu.CompilerParams(
            dimension_semantics=("parallel",), vmem_limit_bytes=VMEM_LIMIT),
        name="out_proj",
    )(y_att, proj2, y_ssm, proj2, x2d, w_out, gain_att, gain_ssm, ln_g, ln_b)


def kernel(x, w_in, conv_w, conv_b, dt_bias, a_log, d_skip, att_norm_g, ssm_norm_g, w_out, ln_g, ln_b):
    bsz, seq, _ = x.shape
    depth = w_in.shape[0]
    assert seq % UNIT == 0
    alpha = (2.0 * depth) ** 0.25
    slopes = jnp.asarray(2.0 ** (-8.0 * np.arange(1, ATT_HEADS + 1) / ATT_HEADS), dtype=F32)
    for layer in range(depth):
        w_main = w_in[layer][:, :D_PROJ].astype(BF16)
        w_dt = jnp.pad(w_in[layer][:, D_PROJ:], ((0, 0), (0, LANES - SSM_HEADS))).astype(BF16)
        x2d = x.reshape(bsz * seq, D_MODEL)
        proj, dt_raw = _in_proj(x2d, w_main, w_dt)
        proj3 = proj.reshape(bsz, seq, D_PROJ)

        y_att = _attention(proj3, slopes)

        src_t, a_cs, w_p = _ssd_decay(
            dt_raw.reshape(bsz, seq, LANES),
            dt_bias[layer].astype(F32).reshape(SSM_HEADS, 1), a_log[layer].astype(F32).reshape(SSM_HEADS, 1))
        y_ssm = _ssd(
            proj3, src_t, a_cs, w_p, conv_w[layer].astype(F32), conv_b[layer].astype(F32).reshape(1, D_XBC),
            jnp.repeat(d_skip[layer].astype(F32), HEAD_DIM).reshape(1, D_SSM))

        out = _out_proj(
            alpha, y_att.reshape(bsz * seq, D_ATT), proj, y_ssm.reshape(bsz * seq, D_SSM), x2d,
            w_out[layer].astype(BF16), att_norm_g[layer].astype(F32).reshape(1, D_ATT),
            ssm_norm_g[layer].astype(F32).reshape(1, D_SSM),
            ln_g[layer].astype(F32).reshape(1, D_MODEL), ln_b[layer].astype(F32).reshape(1, D_MODEL))
        x = out.reshape(bsz, seq, D_MODEL)
    return x
```

```python
import functools

import numpy as np
import jax
import jax.numpy as jnp
from jax import lax
from jax.experimental import pallas as pl
from jax.experimental.pallas import tpu as pltpu

F32 = jnp.float32
BF16 = jnp.bfloat16

D_MODEL = 1024
D_ATT = 1024
HEAD_DIM = 64
ATT_HEADS = D_ATT // HEAD_DIM
ATT_STEPS = 128
DILATIONS = (1, 4, 16)
D_SSM = 1024
SSM_HEADS = D_SSM // HEAD_DIM
SSM_GROUPS = 2
SSM_STATE = 128
CONV_WIDTH = 4
CHUNK = 128
D_BC = 2 * SSM_GROUPS * SSM_STATE
D_PROJ = 4 * D_ATT + D_SSM + D_SSM + D_BC
D_XBC = D_SSM + D_BC
D_XBC_START = D_PROJ - D_XBC
NORM_EPS = 1e-5

LANES = 128
HEAD_PAIRS = ATT_HEADS // 2
UNIT = DILATIONS[-1] * ATT_STEPS
WIDE = DILATIONS[-1]
PADDED_GROUP = WIDE + 4
PADDED_UNIT = UNIT // WIDE * PADDED_GROUP
MIN_DENOMINATOR = 2.0 ** -100
LOG2_E = 1.4426950408889634
Q_SCALE = HEAD_DIM ** -0.5 * LOG2_E
NEG = -1e30
PROJ_ROWS, PROJ_COLS = 2048, 512
CONV_TAIL = 16
OUT_ROWS = 512
DECAY_CHUNKS = 16
VMEM_LIMIT = 56 * 1024 * 1024

COL_Z_ATT = 3
COL_Z_SSM = 4
COL_XS = 5


def _silu(v):
    h = 0.5 * v
    return h + h * jnp.tanh(h)


def _lane_is_first_head():
    return lax.broadcasted_iota(jnp.int32, (1, LANES), 1) < HEAD_DIM


def _in_proj_body(x_ref, w_ref, wdt_ref, proj_ref, dt_ref, xb_ref):
    @pl.when(pl.program_id(1) == 0)
    def _():
        xb_ref[...] = x_ref[...].astype(BF16)
        dt_ref[...] = jnp.dot(xb_ref[...], wdt_ref[...], preferred_element_type=F32)

    def project():
        return jnp.dot(xb_ref[...], w_ref[...], preferred_element_type=F32)

    j = pl.program_id(1)
    is_q = j < D_ATT // PROJ_COLS
    is_gate = jnp.logical_and(j >= COL_Z_ATT * D_ATT // PROJ_COLS, j < COL_XS * D_SSM // PROJ_COLS)

    @pl.when(is_q)
    def _():
        proj_ref[...] = (project() * Q_SCALE).astype(BF16)

    @pl.when(is_gate)
    def _():
        proj_ref[...] = _silu(project()).astype(BF16)

    @pl.when(jnp.logical_not(jnp.logical_or(is_q, is_gate)))
    def _():
        proj_ref[...] = project().astype(BF16)


def _in_proj(x2d, w_main, w_dt):
    m = x2d.shape[0]
    tm, tn = PROJ_ROWS, PROJ_COLS
    assert m % tm == 0
    return pl.pallas_call(
        _in_proj_body,
        grid=(m // tm, D_PROJ // tn),
        in_specs=[
            pl.BlockSpec((tm, D_MODEL), lambda i, j: (jnp.minimum(i + jnp.minimum(j, 1), m // tm - 1), 0)),
            pl.BlockSpec((D_MODEL, tn), lambda i, j: (0, j)),
            pl.BlockSpec((D_MODEL, LANES), lambda i, j: (0, 0)),
        ],
        out_specs=[
            pl.BlockSpec((tm, tn), lambda i, j: (i, j)),
            pl.BlockSpec((tm, LANES), lambda i, j: (i, 0)),
        ],
        out_shape=[
            jax.ShapeDtypeStruct((m, D_PROJ), BF16),
            jax.ShapeDtypeStruct((m, LANES), F32),
        ],
        scratch_shapes=[pltpu.VMEM((tm, D_MODEL), BF16)],
        compiler_params=pltpu.CompilerParams(
            dimension_semantics=("parallel", "arbitrary"), vmem_limit_bytes=VMEM_LIMIT),
        name="in_proj",
    )(x2d, w_main, w_dt)


def _attn_body(slope_ref, ratio_ref, q_ref, k_ref, v_ref, o_ref, qf, kf, vf, qp, kp, vp, acc, m_s, l_s, bias):
    hp, b, u = pl.program_id(0), pl.program_id(1), pl.program_id(2)
    first_head = _lane_is_first_head()
    cur = (u + 1) * UNIT
    cur_padded = (u + 1) * PADDED_UNIT

    @pl.when(jnp.logical_and(b == 0, u == 0))
    def _():
        row = lax.broadcasted_iota(jnp.int32, (ATT_STEPS, 2 * ATT_STEPS), 0)
        col = lax.broadcasted_iota(jnp.int32, (ATT_STEPS, 2 * ATT_STEPS), 1)
        step = row + ATT_STEPS - col
        valid = (step >= 0) & (step <= ATT_STEPS)
        for di, d in enumerate(DILATIONS):
            table = jnp.where(valid, -(slope_ref[2 * hp] * LOG2_E) * (step * d).astype(F32), NEG)
            bias[2 * di] = table
            bias[2 * di + 1] = jnp.where(col < ATT_STEPS, NEG, table)

    @pl.when(jnp.logical_and(hp == 0, jnp.logical_and(b == 0, u == 0)))
    def _():
        kf[0:UNIT, :] = jnp.zeros((UNIT, LANES), F32)
        vf[0:UNIT, :] = jnp.zeros((UNIT, LANES), F32)
        kp[0:PADDED_UNIT, :] = jnp.zeros((PADDED_UNIT, LANES), F32)
        vp[0:PADDED_UNIT, :] = jnp.zeros((PADDED_UNIT, LANES), F32)

    q32, k32, v32 = q_ref[...].astype(F32), k_ref[...].astype(F32), v_ref[...].astype(F32)
    qf[...] = q32
    kf[pl.ds(cur, UNIT), :] = k32
    vf[pl.ds(cur, UNIT), :] = v32
    for g in range(UNIT // WIDE):
        group = slice(g * WIDE, (g + 1) * WIDE)
        qp[g * PADDED_GROUP:g * PADDED_GROUP + WIDE, :] = q32[group, :]
        kp[pl.ds(cur_padded + g * PADDED_GROUP, WIDE), :] = k32[group, :]
        vp[pl.ds(cur_padded + g * PADDED_GROUP, WIDE), :] = v32[group, :]

    def rows(d, start, n):
        return pl.ds(start, n) if d == 1 else pl.ds(start, n, stride=d)

    def stack_heads(q, zero):
        return jnp.concatenate([jnp.where(first_head, q, zero), jnp.where(first_head, zero, q)], axis=0)

    def load_block(d, qs):
        if d == 1:
            def reaching_back(x_ref, xf):
                before = xf[pl.ds(cur - ATT_STEPS, ATT_STEPS), :].astype(BF16)
                return jnp.concatenate([before, x_ref[0:ATT_STEPS, :]], axis=0)

            if isinstance(qs, int):
                q = q_ref[qs:qs + ATT_STEPS, :]
                if qs == 0:
                    kk, vv = reaching_back(k_ref, kf), reaching_back(v_ref, vf)
                else:
                    kk, vv = (x_ref[qs - ATT_STEPS:qs + ATT_STEPS, :] for x_ref in (k_ref, v_ref))
            else:
                q = q_ref[pl.ds(pl.multiple_of(qs, ATT_STEPS), ATT_STEPS), :]
                inside = pl.ds(pl.multiple_of(jnp.maximum(qs - ATT_STEPS, 0), ATT_STEPS), 2 * ATT_STEPS)
                kk = jnp.where(qs == 0, reaching_back(k_ref, kf), k_ref[inside, :])
                vv = jnp.where(qs == 0, reaching_back(v_ref, vf), v_ref[inside, :])
            return stack_heads(q, jnp.zeros_like(q)), kk, vv
        if d == WIDE:
            window = pl.ds(cur_padded - PADDED_UNIT + qs, 2 * ATT_STEPS, stride=PADDED_GROUP)
            q2 = stack_heads(qp[pl.ds(qs, ATT_STEPS, stride=PADDED_GROUP), :], 0.0).astype(BF16)
            return q2, kp[window, :].astype(BF16), vp[window, :].astype(BF16)
        window = rows(d, cur + qs - ATT_STEPS * d, 2 * ATT_STEPS)
        q2 = stack_heads(qf[rows(d, qs, ATT_STEPS), :], 0.0).astype(BF16)
        return q2, kf[window, :].astype(BF16), vf[window, :].astype(BF16)

    def block_scores(di, d, qs, first):
        q2, kk, vv = load_block(d, qs)
        s = lax.dot_general(q2, kk, (((1,), (1,)), ((), ())), preferred_element_type=F32)
        table = bias[2 * di + first.astype(jnp.int32)]
        return s + jnp.concatenate([table, table * ratio_ref[hp]], axis=0), vv

    def weighted_values(p, vv):
        vext = jnp.concatenate([vv, jnp.ones_like(vv)], axis=1)
        oe = jnp.dot(p, vext, preferred_element_type=F32)
        o_blk = jnp.where(first_head, oe[:ATT_STEPS, :LANES], oe[ATT_STEPS:, :LANES])
        l_blk = jnp.where(first_head, oe[:ATT_STEPS, LANES:], oe[ATT_STEPS:, LANES:])
        return o_blk, l_blk

    def block_start(d, t):
        blk = t // d
        r = t - blk * d
        return blk * (ATT_STEPS * d) + r, jnp.logical_and(u == 0, blk == 0)

    n_sub = UNIT // ATT_STEPS
    order = tuple(reversed(range(len(DILATIONS))))

    for di in order:
        d = DILATIONS[di]
        for t in range(n_sub):
            qs, first = block_start(d, t)
            s, vv = block_scores(di, d, qs, first)
            o_blk, l_blk = weighted_values(jnp.exp2(s).astype(BF16), vv)
            rows_q = rows(d, qs, ATT_STEPS)
            if di == order[0]:
                acc[rows_q, :] = o_blk
                l_s[rows_q, :] = l_blk
            else:
                acc[rows_q, :] += o_blk
                l_s[rows_q, :] += l_blk

    den = l_s[...]
    in_range = (den >= MIN_DENOMINATOR) & (den < jnp.inf) & (jnp.abs(acc[...]) < jnp.inf)
    n_bad = jnp.max(jnp.max(jnp.where(in_range, 0.0, 1.0), axis=1, keepdims=True), axis=0, keepdims=True)

    @pl.when(n_bad[0, 0] > 0.0)
    def _():
        for di in order:
            d = DILATIONS[di]

            def body(t, carry, di=di, d=d):
                qs, first = block_start(d, t)
                s, vv = block_scores(di, d, qs, first)
                mb = jnp.max(s, axis=1, keepdims=True)
                o_blk, l_blk = weighted_values(jnp.exp2(s - mb).astype(BF16), vv)
                m_blk = jnp.where(first_head, mb[:ATT_STEPS], mb[ATT_STEPS:])
                rows_q = rows(d, qs, ATT_STEPS)
                if di == order[0]:
                    acc[rows_q, :] = o_blk
                    l_s[rows_q, :] = l_blk
                    m_s[rows_q, :] = m_blk
                else:
                    m_old = m_s[rows_q, :]
                    m_new = jnp.maximum(m_old, m_blk)
                    c_old = jnp.exp2(m_old - m_new)
                    c_blk = jnp.exp2(m_blk - m_new)
                    acc[rows_q, :] = acc[rows_q, :] * c_old + o_blk * c_blk
                    l_s[rows_q, :] = l_s[rows_q, :] * c_old + l_blk * c_blk
                    m_s[rows_q, :] = m_new
                return carry
            lax.fori_loop(0, n_sub, body, 0)

    o_ref[...] = (acc[...] / l_s[...]).astype(BF16)


def _attention(proj3, slopes):
    bsz, seq, _ = proj3.shape
    n_units = seq // UNIT
    blk = (None, UNIT, LANES)
    section = lambda k: pl.BlockSpec(blk, lambda h, b, u: (b, u, k * HEAD_PAIRS + h))
    return pl.pallas_call(
        _attn_body,
        grid=(HEAD_PAIRS, bsz, n_units),
        in_specs=[pl.BlockSpec(memory_space=pltpu.SMEM), pl.BlockSpec(memory_space=pltpu.SMEM),
                  section(0), section(1), section(2)],
        out_specs=section(0),
        out_shape=jax.ShapeDtypeStruct((bsz, seq, D_ATT), BF16),
        scratch_shapes=[
            pltpu.VMEM((UNIT, LANES), F32),
            pltpu.VMEM(((n_units + 1) * UNIT, LANES), F32),
            pltpu.VMEM(((n_units + 1) * UNIT, LANES), F32),
            pltpu.VMEM((PADDED_UNIT, LANES), F32),
            pltpu.VMEM(((n_units + 1) * PADDED_UNIT, LANES), F32),
            pltpu.VMEM(((n_units + 1) * PADDED_UNIT, LANES), F32),
            pltpu.VMEM((UNIT, LANES), F32),
            pltpu.VMEM((UNIT, LANES), F32),
            pltpu.VMEM((UNIT, LANES), F32),
            pltpu.VMEM((2 * len(DILATIONS), ATT_STEPS, 2 * ATT_STEPS), F32),
        ],
        compiler_params=pltpu.CompilerParams(
            dimension_semantics=("arbitrary", "arbitrary", "arbitrary"), vmem_limit_bytes=VMEM_LIMIT),
        name="dilated_attention",
    )(slopes, slopes[1::2] / slopes[0::2], proj3, proj3, proj3)


def _split_bf16(v, parts):
    out = []
    for _ in range(parts):
        hi = v.astype(BF16)
        out.append(hi)
        v = v - hi.astype(F32)
    return out


def _ssd_decay_body(dt_ref, dtb_ref, alog_ref, src_t_ref, acs_ref, w_ref):
    ri = lax.broadcasted_iota(jnp.int32, (CHUNK, CHUNK), 0)
    ci = lax.broadcasted_iota(jnp.int32, (CHUNK, CHUNK), 1)
    triu_b = jnp.where(ri <= ci, 1.0, 0.0).astype(BF16)
    pad_rows = jnp.zeros((CHUNK - SSM_HEADS, CHUNK), F32)
    chunk_rows = [slice(g * CHUNK, (g + 1) * CHUNK) for g in range(DECAY_CHUNKS)]
    per_chunk = lambda v: jnp.concatenate([v] * DECAY_CHUNKS, axis=0)
    dt_in = jnp.concatenate([dt_ref[r, :].T[0:SSM_HEADS, :] for r in chunk_rows], axis=0) + per_chunk(dtb_ref[...])
    dt_t = jnp.maximum(dt_in, 0.0) + jnp.log1p(jnp.exp(-jnp.abs(dt_in)))
    adt_t = dt_t * per_chunk(-jnp.exp(alog_ref[...]))
    a_cs_t = sum(jnp.dot(part, triu_b, preferred_element_type=F32) for part in _split_bf16(adt_t, 3))
    w_t = dt_t * jnp.exp(a_cs_t[:, CHUNK - 1:CHUNK] - a_cs_t)
    src_t = a_cs_t - jnp.log(dt_t)
    for g, r in enumerate(chunk_rows):
        heads = slice(g * SSM_HEADS, (g + 1) * SSM_HEADS)
        src_t_ref[g] = src_t[heads, :]
        acs_ref[r, :] = jnp.concatenate([a_cs_t[heads, :], pad_rows], axis=0).T
        w_ref[r, :] = jnp.concatenate([w_t[heads, :], pad_rows], axis=0).T


def _ssd_decay(dt3, dtb, alog):
    bsz, seq, _ = dt3.shape
    n_chunks = seq // CHUNK
    rows = DECAY_CHUNKS * CHUNK
    full = lambda shape: pl.BlockSpec(shape, lambda b, i: (0,) * len(shape))
    pos_major = pl.BlockSpec((None, rows, LANES), lambda b, i: (b, i, 0))
    return pl.pallas_call(
        _ssd_decay_body,
        grid=(bsz, n_chunks // DECAY_CHUNKS),
        in_specs=[pos_major, full((SSM_HEADS, 1)), full((SSM_HEADS, 1))],
        out_specs=[
            pl.BlockSpec((None, DECAY_CHUNKS, SSM_HEADS, CHUNK), lambda b, i: (b, i, 0, 0)),
            pos_major, pos_major,
        ],
        out_shape=[
            jax.ShapeDtypeStruct((bsz, n_chunks, SSM_HEADS, CHUNK), F32),
            jax.ShapeDtypeStruct((bsz, seq, LANES), F32),
            jax.ShapeDtypeStruct((bsz, seq, LANES), F32),
        ],
        compiler_params=pltpu.CompilerParams(
            dimension_semantics=("parallel", "parallel"), vmem_limit_bytes=VMEM_LIMIT),
        name="ssd_decay",
    )(dt3, dtb, alog)


def _ssd_body(xs_ref, bc_ref, src_t_ref, acs_ref, wp_ref, sel_ref, cw_ref, cb_ref, dskip_ref, o_ref,
              upad, tail, state):
    @pl.when(pl.program_id(1) == 0)
    def _():
        upad[0:CHUNK, :] = jnp.zeros((CHUNK, D_XBC), BF16)
        tail[...] = jnp.zeros_like(tail)
        state[...] = jnp.zeros_like(state)

    upad[CHUNK - CONV_TAIL:CHUNK, :] = tail[...]
    upad[CHUNK:2 * CHUNK, 0:D_SSM] = xs_ref[...]
    upad[CHUNK:2 * CHUNK, D_SSM:D_XBC] = bc_ref[...]

    def conv_silu(c0, c1):
        shifted = jnp.dot(sel_ref[...], upad[:, c0:c1], preferred_element_type=F32)
        conv = cb_ref[:, c0:c1] + cw_ref[CONV_WIDTH - 1:CONV_WIDTH, c0:c1] * upad[CHUNK:2 * CHUNK, c0:c1].astype(F32)
        for tap in range(CONV_WIDTH - 1):
            conv = conv + cw_ref[tap:tap + 1, c0:c1] * shifted[tap * CHUNK:(tap + 1) * CHUNK, :]
        return _silu(conv)

    bc = conv_silu(D_SSM, D_XBC)
    tail[:, 0:D_SSM] = xs_ref[CHUNK - CONV_TAIL:CHUNK, :]
    tail[:, D_SSM:D_XBC] = bc_ref[CHUNK - CONV_TAIL:CHUNK, :]

    first_head = _lane_is_first_head()
    a_cs = acs_ref[...]
    w_p = wp_ref[...]
    src_t = src_t_ref[...]
    ri = lax.broadcasted_iota(jnp.int32, (CHUNK, CHUNK), 0)
    ci = lax.broadcasted_iota(jnp.int32, (CHUNK, CHUNK), 1)
    tril = ri >= ci

    def spread(v, pairs):
        return jnp.concatenate(
            [jnp.where(first_head, v[:, 2 * p:2 * p + 1], v[:, 2 * p + 1:2 * p + 2]) for p in pairs], axis=1)

    pairs_per_group = HEAD_PAIRS // SSM_GROUPS
    for g in range(SSM_GROUPS):
        pairs = range(g * pairs_per_group, (g + 1) * pairs_per_group)
        gcols = slice(pairs[0] * LANES, (pairs[-1] + 1) * LANES)
        xs = conv_silu(gcols.start, gcols.stop)
        xs_b = xs.astype(BF16)
        bg = bc[:, g * SSM_STATE:(g + 1) * SSM_STATE]
        cg = bc[:, (SSM_GROUPS + g) * SSM_STATE:(SSM_GROUPS + g + 1) * SSM_STATE].astype(BF16)
        cb = lax.dot_general(cg, bg.astype(BF16), (((1,), (1,)), ((), ())), preferred_element_type=F32)
        prev = state[:, gcols]
        y_off = jnp.dot(cg, prev.astype(BF16), preferred_element_type=F32)
        for i, p in enumerate(pairs):
            cols = slice(p * LANES, (p + 1) * LANES)
            local = slice(i * LANES, (i + 1) * LANES)
            xs_p = xs_b[:, local]
            zero = jnp.zeros_like(xs_p)
            m_pair = []
            for h in (2 * p, 2 * p + 1):
                seg = a_cs[:, h:h + 1] - src_t[h:h + 1, :]
                m_pair.append((cb * jnp.exp(jnp.where(tril, seg, NEG))).astype(BF16))
            rhs = jnp.concatenate([jnp.where(first_head, xs_p, zero), jnp.where(first_head, zero, xs_p)], axis=0)
            y = jnp.dot(jnp.concatenate(m_pair, axis=1), rhs, preferred_element_type=F32)
            y = y + y_off[:, local] * jnp.exp(spread(a_cs, [p])) + dskip_ref[:, cols] * xs[:, local]
            o_ref[:, cols] = y.astype(BF16)
        xw = (xs * spread(w_p, pairs)).astype(BF16)
        new = jnp.dot(bg.T.astype(BF16), xw, preferred_element_type=F32)
        state[:, gcols] = prev * jnp.exp(spread(a_cs[CHUNK - 1:CHUNK, :], pairs)) + new


def _ssd(proj3, src_t, a_cs, w_p, conv_w, conv_b, dskip):
    bsz, seq, _ = proj3.shape
    sel = np.zeros(((CONV_WIDTH - 1) * CHUNK, 2 * CHUNK), np.float32)
    for tap in range(CONV_WIDTH - 1):
        sel[tap * CHUNK + np.arange(CHUNK), CHUNK - (CONV_WIDTH - 1) + tap + np.arange(CHUNK)] = 1.0
    full = lambda shape: pl.BlockSpec(shape, lambda b, c: (0,) * len(shape))
    pos_major = pl.BlockSpec((None, CHUNK, LANES), lambda b, c: (b, c, 0))
    return pl.pallas_call(
        _ssd_body,
        grid=(bsz, seq // CHUNK),
        in_specs=[
            pl.BlockSpec((None, CHUNK, D_SSM), lambda b, c: (b, c, COL_XS)),
            pl.BlockSpec((None, CHUNK, D_BC), lambda b, c: (b, c, (COL_XS + 1) * D_SSM // D_BC)),
            pl.BlockSpec((None, None, SSM_HEADS, CHUNK), lambda b, c: (b, c, 0, 0)),
            pos_major, pos_major,
            full(sel.shape), full((CONV_WIDTH, D_XBC)), full((1, D_XBC)), full((1, D_SSM)),
        ],
        out_specs=pl.BlockSpec((None, CHUNK, D_SSM), lambda b, c: (b, c, 0)),
        out_shape=jax.ShapeDtypeStruct((bsz, seq, D_SSM), BF16),
        scratch_shapes=[
            pltpu.VMEM((2 * CHUNK, D_XBC), BF16),
            pltpu.VMEM((CONV_TAIL, D_XBC), BF16),
            pltpu.VMEM((SSM_STATE, D_SSM), F32),
        ],
        compiler_params=pltpu.CompilerParams(
            dimension_semantics=("parallel", "arbitrary"), vmem_limit_bytes=VMEM_LIMIT),
        name="ssd",
    )(proj3, proj3, src_t, a_cs, w_p, jnp.asarray(sel, BF16), conv_w, conv_b, dskip)


def _out_body(alpha, ya_ref, za_ref, ys_ref, zs_ref, x_ref, w_ref, ga_ref, gs_ref, lg_ref, lb_ref, o_ref):
    def gated_rms_norm(y_ref, gate_ref, g_ref):
        y = y_ref[...].astype(F32) * gate_ref[...].astype(F32)
        return (y * lax.rsqrt(jnp.mean(y * y, axis=-1, keepdims=True) + NORM_EPS) * g_ref[...]).astype(BF16)

    mix = jnp.concatenate([gated_rms_norm(ya_ref, za_ref, ga_ref), gated_rms_norm(ys_ref, zs_ref, gs_ref)], axis=1)
    h = alpha * x_ref[...] + jnp.dot(mix, w_ref[...], preferred_element_type=F32)
    mu = jnp.mean(h, axis=-1, keepdims=True)
    hc = h - mu
    var = jnp.mean(hc * hc, axis=-1, keepdims=True)
    o_ref[...] = hc * lax.rsqrt(var + NORM_EPS) * lg_ref[...] + lb_ref[...]


def _out_proj(alpha, y_att, proj2, y_ssm, x2d, w_out, gain_att, gain_ssm, ln_g, ln_b):
    m = x2d.shape[0]
    tm = OUT_ROWS
    rows = lambda shape: pl.BlockSpec(shape, lambda i: (i, 0))
    full = lambda shape: pl.BlockSpec(shape, lambda i: (0, 0))
    return pl.pallas_call(
        functools.partial(_out_body, alpha),
        grid=(m // tm,),
        in_specs=[
            rows((tm, D_ATT)),
            pl.BlockSpec((tm, D_ATT), lambda i: (i, COL_Z_ATT)),
            rows((tm, D_SSM)),
            pl.BlockSpec((tm, D_SSM), lambda i: (i, COL_Z_SSM)),
            rows((tm, D_MODEL)),
            full((D_ATT + D_SSM, D_MODEL)),
            full((1, D_ATT)), full((1, D_SSM)), full((1, D_MODEL)), full((1, D_MODEL)),
        ],
        out_specs=rows((tm, D_MODEL)),
        out_shape=jax.ShapeDtypeStruct((m, D_MODEL), F32),
        compiler_params=pltpu.CompilerParams(
            dimension_semantics=("parallel",), vmem_limit_bytes=VMEM_LIMIT),
        name="out_proj",
    )(y_att, proj2, y_ssm, proj2, x2d, w_out, gain_att, gain_ssm, ln_g, ln_b)


def kernel(x, w_in, conv_w, conv_b, dt_bias, a_log, d_skip, att_norm_g, ssm_norm_g, w_out, ln_g, ln_b):
    bsz, seq, _ = x.shape
    depth = w_in.shape[0]
    assert seq % UNIT == 0
    alpha = (2.0 * depth) ** 0.25
    slopes = jnp.asarray(2.0 ** (-8.0 * np.arange(1, ATT_HEADS + 1) / ATT_HEADS), dtype=F32)
    for layer in range(depth):
        w_main = w_in[layer][:, :D_PROJ].astype(BF16)
        w_dt = jnp.pad(w_in[layer][:, D_PROJ:], ((0, 0), (0, LANES - SSM_HEADS))).astype(BF16)
        x2d = x.reshape(bsz * seq, D_MODEL)
        proj, dt_raw = _in_proj(x2d, w_main, w_dt)
        proj3 = proj.reshape(bsz, seq, D_PROJ)

        y_att = _attention(proj3, slopes)

        src_t, a_cs, w_p = _ssd_decay(
            dt_raw.reshape(bsz, seq, LANES),
            dt_bias[layer].astype(F32).reshape(SSM_HEADS, 1), a_log[layer].astype(F32).reshape(SSM_HEADS, 1))
        y_ssm = _ssd(
            proj3, src_t, a_cs, w_p, conv_w[layer].astype(F32), conv_b[layer].astype(F32).reshape(1, D_XBC),
            jnp.repeat(d_skip[layer].astype(F32), HEAD_DIM).reshape(1, D_SSM))

        out = _out_proj(
            alpha, y_att.reshape(bsz * seq, D_ATT), proj, y_ssm.reshape(bsz * seq, D_SSM), x2d,
            w_out[layer].astype(BF16), att_norm_g[layer].astype(F32).reshape(1, D_ATT),
            ssm_norm_g[layer].astype(F32).reshape(1, D_SSM),
            ln_g[layer].astype(F32).reshape(1, D_MODEL), ln_b[layer].astype(F32).reshape(1, D_MODEL))
        x = out.reshape(bsz, seq, D_MODEL)
    return x
```

```python
import functools

import numpy as np
import jax
import jax.numpy as jnp
from jax import lax
from jax.experimental import pallas as pl
from jax.experimental.pallas import tpu as pltpu

F32 = jnp.float32
BF16 = jnp.bfloat16

D_MODEL = 1024
D_ATT = 1024
HEAD_DIM = 64
ATT_HEADS = D_ATT // HEAD_DIM
ATT_STEPS = 128
DILATIONS = (1, 4, 16)
D_SSM = 1024
SSM_HEADS = D_SSM // HEAD_DIM
SSM_GROUPS = 2
SSM_STATE = 128
CONV_WIDTH = 4
CHUNK = 128
D_BC = 2 * SSM_GROUPS * SSM_STATE
D_PROJ = 4 * D_ATT + D_SSM + D_SSM + D_BC
D_XBC = D_SSM + D_BC
NORM_EPS = 1e-5

LANES = 128
HEAD_PAIRS = ATT_HEADS // 2
UNIT = DILATIONS[-1] * ATT_STEPS
WIDE = DILATIONS[-1]
PADDED_GROUP = WIDE + 4
PADDED_UNIT = UNIT // WIDE * PADDED_GROUP
MIN_DENOMINATOR = 2.0 ** -100
LOG2_E = 1.4426950408889634
Q_SCALE = HEAD_DIM ** -0.5 * LOG2_E
NEG = -1e30
PROJ_ROWS, PROJ_COLS = 2048, 512
CONV_TAIL = 16
OUT_ROWS = 1024
DECAY_CHUNKS = 16
VMEM_LIMIT = 56 * 1024 * 1024

COL_Z_ATT = 3
COL_Z_SSM = 4
COL_XS = 5


def _silu(v):
    h = 0.5 * v
    return h + h * jnp.tanh(h)


def _lane_is_first_head():
    return lax.broadcasted_iota(jnp.int32, (1, LANES), 1) < HEAD_DIM


def _in_proj_body(x_ref, w_ref, wdt_ref, proj_ref, dt_ref, xb_ref):
    @pl.when(pl.program_id(1) == 0)
    def _():
        xb_ref[...] = x_ref[...].astype(BF16)
        dt_ref[...] = jnp.dot(xb_ref[...], wdt_ref[...], preferred_element_type=F32)

    def project():
        return jnp.dot(xb_ref[...], w_ref[...], preferred_element_type=F32)

    j = pl.program_id(1)
    is_q = j < D_ATT // PROJ_COLS
    is_gate = jnp.logical_and(j >= COL_Z_ATT * D_ATT // PROJ_COLS, j < COL_XS * D_SSM // PROJ_COLS)

    @pl.when(is_q)
    def _():
        proj_ref[...] = (project() * Q_SCALE).astype(BF16)

    @pl.when(is_gate)
    def _():
        proj_ref[...] = _silu(project()).astype(BF16)

    @pl.when(jnp.logical_not(jnp.logical_or(is_q, is_gate)))
    def _():
        proj_ref[...] = project().astype(BF16)


def _in_proj(x2d, w_main, w_dt):
    m = x2d.shape[0]
    tm, tn = PROJ_ROWS, PROJ_COLS
    assert m % tm == 0
    return pl.pallas_call(
        _in_proj_body,
        grid=(m // tm, D_PROJ // tn),
        in_specs=[
            pl.BlockSpec((tm, D_MODEL), lambda i, j: (jnp.minimum(i + jnp.minimum(j, 1), m // tm - 1), 0)),
            pl.BlockSpec((D_MODEL, tn), lambda i, j: (0, j)),
            pl.BlockSpec((D_MODEL, LANES), lambda i, j: (0, 0)),
        ],
        out_specs=[
            pl.BlockSpec((tm, tn), lambda i, j: (i, j)),
            pl.BlockSpec((tm, LANES), lambda i, j: (i, 0)),
        ],
        out_shape=[
            jax.ShapeDtypeStruct((m, D_PROJ), BF16),
            jax.ShapeDtypeStruct((m, LANES), F32),
        ],
        scratch_shapes=[pltpu.VMEM((tm, D_MODEL), BF16)],
        compiler_params=pltpu.CompilerParams(
            dimension_semantics=("parallel", "arbitrary"), vmem_limit_bytes=VMEM_LIMIT),
        name="in_proj",
    )(x2d, w_main, w_dt)


def _attn_body(slope_ref, ratio_ref, q_ref, k_ref, v_ref, o_ref, qf, kf, vf, qp, kp, vp, acc, m_s, l_s, bias):
    hp, b, u = pl.program_id(0), pl.program_id(1), pl.program_id(2)
    first_head = _lane_is_first_head()
    cur = (u + 1) * UNIT
    cur_padded = (u + 1) * PADDED_UNIT

    @pl.when(jnp.logical_and(b == 0, u == 0))
    def _():
        row = lax.broadcasted_iota(jnp.int32, (ATT_STEPS, 2 * ATT_STEPS), 0)
        col = lax.broadcasted_iota(jnp.int32, (ATT_STEPS, 2 * ATT_STEPS), 1)
        step = row + ATT_STEPS - col
        valid = (step >= 0) & (step <= ATT_STEPS)
        for di, d in enumerate(DILATIONS):
            table = jnp.where(valid, -(slope_ref[2 * hp] * LOG2_E) * (step * d).astype(F32), NEG)
            bias[2 * di] = table
            bias[2 * di + 1] = jnp.where(col < ATT_STEPS, NEG, table)

    @pl.when(jnp.logical_and(hp == 0, jnp.logical_and(b == 0, u == 0)))
    def _():
        kf[0:UNIT, :] = jnp.zeros((UNIT, LANES), F32)
        vf[0:UNIT, :] = jnp.zeros((UNIT, LANES), F32)
        kp[0:PADDED_UNIT, :] = jnp.zeros((PADDED_UNIT, LANES), F32)
        vp[0:PADDED_UNIT, :] = jnp.zeros((PADDED_UNIT, LANES), F32)

    q32, k32, v32 = q_ref[...].astype(F32), k_ref[...].astype(F32), v_ref[...].astype(F32)
    qf[...] = q32
    kf[pl.ds(cur, UNIT), :] = k32
    vf[pl.ds(cur, UNIT), :] = v32
    for g in range(UNIT // WIDE):
        group = slice(g * WIDE, (g + 1) * WIDE)
        qp[g * PADDED_GROUP:g * PADDED_GROUP + WIDE, :] = q32[group, :]
        kp[pl.ds(cur_padded + g * PADDED_GROUP, WIDE), :] = k32[group, :]
        vp[pl.ds(cur_padded + g * PADDED_GROUP, WIDE), :] = v32[group, :]

    def rows(d, start, n):
        return pl.ds(start, n) if d == 1 else pl.ds(start, n, stride=d)

    def stack_heads(q, zero):
        return jnp.concatenate([jnp.where(first_head, q, zero), jnp.where(first_head, zero, q)], axis=0)

    def load_block(d, qs):
        if d == 1:
            def reaching_back(x_ref, xf):
                before = xf[pl.ds(cur - ATT_STEPS, ATT_STEPS), :].astype(BF16)
                return jnp.concatenate([before, x_ref[0:ATT_STEPS, :]], axis=0)

            if isinstance(qs, int):
                q = q_ref[qs:qs + ATT_STEPS, :]
                if qs == 0:
                    kk, vv = reaching_back(k_ref, kf), reaching_back(v_ref, vf)
                else:
                    kk, vv = (x_ref[qs - ATT_STEPS:qs + ATT_STEPS, :] for x_ref in (k_ref, v_ref))
            else:
                q = q_ref[pl.ds(pl.multiple_of(qs, ATT_STEPS), ATT_STEPS), :]
                inside = pl.ds(pl.multiple_of(jnp.maximum(qs - ATT_STEPS, 0), ATT_STEPS), 2 * ATT_STEPS)
                kk = jnp.where(qs == 0, reaching_back(k_ref, kf), k_ref[inside, :])
                vv = jnp.where(qs == 0, reaching_back(v_ref, vf), v_ref[inside, :])
            return stack_heads(q, jnp.zeros_like(q)), kk, vv
        if d == WIDE:
            window = pl.ds(cur_padded - PADDED_UNIT + qs, 2 * ATT_STEPS, stride=PADDED_GROUP)
            q2 = stack_heads(qp[pl.ds(qs, ATT_STEPS, stride=PADDED_GROUP), :], 0.0).astype(BF16)
            return q2, kp[window, :].astype(BF16), vp[window, :].astype(BF16)
        window = rows(d, cur + qs - ATT_STEPS * d, 2 * ATT_STEPS)
        q2 = stack_heads(qf[rows(d, qs, ATT_STEPS), :], 0.0).astype(BF16)
        return q2, kf[window, :].astype(BF16), vf[window, :].astype(BF16)

    def block_scores(di, d, qs, first):
        q2, kk, vv = load_block(d, qs)
        s = lax.dot_general(q2, kk, (((1,), (1,)), ((), ())), preferred_element_type=F32)
        table = bias[2 * di + first.astype(jnp.int32)]
        return s + jnp.concatenate([table, table * ratio_ref[hp]], axis=0), vv

    def weighted_values(p, vv):
        vext = jnp.concatenate([vv, jnp.ones_like(vv)], axis=1)
        oe = jnp.dot(p, vext, preferred_element_type=F32)
        o_blk = jnp.where(first_head, oe[:ATT_STEPS, :LANES], oe[ATT_STEPS:, :LANES])
        l_blk = jnp.where(first_head, oe[:ATT_STEPS, LANES:], oe[ATT_STEPS:, LANES:])
        return o_blk, l_blk

    def block_start(d, t):
        blk = t // d
        r = t - blk * d
        return blk * (ATT_STEPS * d) + r, jnp.logical_and(u == 0, blk == 0)

    n_sub = UNIT // ATT_STEPS
    order = tuple(reversed(range(len(DILATIONS))))

    for di in order:
        d = DILATIONS[di]
        for t in range(n_sub):
            qs, first = block_start(d, t)
            s, vv = block_scores(di, d, qs, first)
            o_blk, l_blk = weighted_values(jnp.exp2(s).astype(BF16), vv)
            rows_q = rows(d, qs, ATT_STEPS)
            if di == order[0]:
                acc[rows_q, :] = o_blk
                l_s[rows_q, :] = l_blk
            else:
                acc[rows_q, :] += o_blk
                l_s[rows_q, :] += l_blk

    den = l_s[...]
    in_range = (den >= MIN_DENOMINATOR) & (den < jnp.inf) & (jnp.abs(acc[...]) < jnp.inf)
    n_bad = jnp.max(jnp.max(jnp.where(in_range, 0.0, 1.0), axis=1, keepdims=True), axis=0, keepdims=True)

    @pl.when(n_bad[0, 0] > 0.0)
    def _():
        for di in order:
            d = DILATIONS[di]

            def body(t, carry, di=di, d=d):
                qs, first = block_start(d, t)
                s, vv = block_scores(di, d, qs, first)
                mb = jnp.max(s, axis=1, keepdims=True)
                o_blk, l_blk = weighted_values(jnp.exp2(s - mb).astype(BF16), vv)
                m_blk = jnp.where(first_head, mb[:ATT_STEPS], mb[ATT_STEPS:])
                rows_q = rows(d, qs, ATT_STEPS)
                if di == order[0]:
                    acc[rows_q, :] = o_blk
                    l_s[rows_q, :] = l_blk
                    m_s[rows_q, :] = m_blk
                else:
                    m_old = m_s[rows_q, :]
                    m_new = jnp.maximum(m_old, m_blk)
                    c_old = jnp.exp2(m_old - m_new)
                    c_blk = jnp.exp2(m_blk - m_new)
                    acc[rows_q, :] = acc[rows_q, :] * c_old + o_blk * c_blk
                    l_s[rows_q, :] = l_s[rows_q, :] * c_old + l_blk * c_blk
                    m_s[rows_q, :] = m_new
                return carry
            lax.fori_loop(0, n_sub, body, 0)

    o_ref[...] = (acc[...] / l_s[...]).astype(BF16)


def _attention(proj3, slopes):
    bsz, seq, _ = proj3.shape
    n_units = seq // UNIT
    blk = (None, UNIT, LANES)
    section = lambda k: pl.BlockSpec(blk, lambda h, b, u: (b, u, k * HEAD_PAIRS + h))
    return pl.pallas_call(
        _attn_body,
        grid=(HEAD_PAIRS, bsz, n_units),
        in_specs=[pl.BlockSpec(memory_space=pltpu.SMEM), pl.BlockSpec(memory_space=pltpu.SMEM),
                  section(0), section(1), section(2)],
        out_specs=section(0),
        out_shape=jax.ShapeDtypeStruct((bsz, seq, D_ATT), BF16),
        scratch_shapes=[
            pltpu.VMEM((UNIT, LANES), F32),
            pltpu.VMEM(((n_units + 1) * UNIT, LANES), F32),
            pltpu.VMEM(((n_units + 1) * UNIT, LANES), F32),
            pltpu.VMEM((PADDED_UNIT, LANES), F32),
            pltpu.VMEM(((n_units + 1) * PADDED_UNIT, LANES), F32),
            pltpu.VMEM(((n_units + 1) * PADDED_UNIT, LANES), F32),
            pltpu.VMEM((UNIT, LANES), F32),
            pltpu.VMEM((UNIT, LANES), F32),
            pltpu.VMEM((UNIT, LANES), F32),
            pltpu.VMEM((2 * len(DILATIONS), ATT_STEPS, 2 * ATT_STEPS), F32),
        ],
        compiler_params=pltpu.CompilerParams(
            dimension_semantics=("arbitrary", "arbitrary", "arbitrary"), vmem_limit_bytes=VMEM_LIMIT),
        name="dilated_attention",
    )(slopes, slopes[1::2] / slopes[0::2], proj3, proj3, proj3)


def _split_bf16(v, parts):
    out = []
    for _ in range(parts):
        hi = v.astype(BF16)
        out.append(hi)
        v = v - hi.astype(F32)
    return out


def _ssd_decay_body(dt_ref, dtb_ref, alog_ref, src_t_ref, acs_ref, w_ref):
    ri = lax.broadcasted_iota(jnp.int32, (CHUNK, CHUNK), 0)
    ci = lax.broadcasted_iota(jnp.int32, (CHUNK, CHUNK), 1)
    triu_b = jnp.where(ri <= ci, 1.0, 0.0).astype(BF16)
    pad_rows = jnp.zeros((CHUNK - SSM_HEADS, CHUNK), F32)
    chunk_rows = [slice(g * CHUNK, (g + 1) * CHUNK) for g in range(DECAY_CHUNKS)]
    per_chunk = lambda v: jnp.concatenate([v] * DECAY_CHUNKS, axis=0)
    dt_in = jnp.concatenate([dt_ref[r, :].T[0:SSM_HEADS, :] for r in chunk_rows], axis=0) + per_chunk(dtb_ref[...])
    dt_t = jnp.maximum(dt_in, 0.0) + jnp.log1p(jnp.exp(-jnp.abs(dt_in)))
    adt_t = dt_t * per_chunk(-jnp.exp(alog_ref[...]))
    a_cs_t = sum(jnp.dot(part, triu_b, preferred_element_type=F32) for part in _split_bf16(adt_t, 3))
    w_t = dt_t * jnp.exp(a_cs_t[:, CHUNK - 1:CHUNK] - a_cs_t)
    src_t = a_cs_t - jnp.log(dt_t)
    for g, r in enumerate(chunk_rows):
        heads = slice(g * SSM_HEADS, (g + 1) * SSM_HEADS)
        src_t_ref[g] = src_t[heads, :]
        acs_ref[r, :] = jnp.concatenate([a_cs_t[heads, :], pad_rows], axis=0).T
        w_ref[r, :] = jnp.concatenate([w_t[heads, :], pad_rows], axis=0).T


def _ssd_decay(dt3, dtb, alog):
    bsz, seq, _ = dt3.shape
    n_chunks = seq // CHUNK
    rows = DECAY_CHUNKS * CHUNK
    full = lambda shape: pl.BlockSpec(shape, lambda b, i: (0,) * len(shape))
    pos_major = pl.BlockSpec((None, rows, LANES), lambda b, i: (b, i, 0))
    return pl.pallas_call(
        _ssd_decay_body,
        grid=(bsz, n_chunks // DECAY_CHUNKS),
        in_specs=[pos_major, full((SSM_HEADS, 1)), full((SSM_HEADS, 1))],
        out_specs=[
            pl.BlockSpec((None, DECAY_CHUNKS, SSM_HEADS, CHUNK), lambda b, i: (b, i, 0, 0)),
            pos_major, pos_major,
        ],
        out_shape=[
            jax.ShapeDtypeStruct((bsz, n_chunks, SSM_HEADS, CHUNK), F32),
            jax.ShapeDtypeStruct((bsz, seq, LANES), F32),
            jax.ShapeDtypeStruct((bsz, seq, LANES), F32),
        ],
        compiler_params=pltpu.CompilerParams(
            dimension_semantics=("parallel", "parallel"), vmem_limit_bytes=VMEM_LIMIT),
        name="ssd_decay",
    )(dt3, dtb, alog)


def _ssd_body(xs_ref, bc_ref, src_t_ref, acs_ref, wp_ref, sel_ref, cw_ref, cb_ref, dskip_ref, o_ref,
              upad, tail, state):
    @pl.when(pl.program_id(1) == 0)
    def _():
        upad[0:CHUNK, :] = jnp.zeros((CHUNK, D_XBC), BF16)
        tail[...] = jnp.zeros_like(tail)
        state[...] = jnp.zeros_like(state)

    upad[CHUNK - CONV_TAIL:CHUNK, :] = tail[...]
    upad[CHUNK:2 * CHUNK, 0:D_SSM] = xs_ref[...]
    upad[CHUNK:2 * CHUNK, D_SSM:D_XBC] = bc_ref[...]

    def conv_silu(c0, c1):
        shifted = jnp.dot(sel_ref[...], upad[:, c0:c1], preferred_element_type=F32)
        conv = cb_ref[:, c0:c1] + cw_ref[CONV_WIDTH - 1:CONV_WIDTH, c0:c1] * upad[CHUNK:2 * CHUNK, c0:c1].astype(F32)
        for tap in range(CONV_WIDTH - 1):
            conv = conv + cw_ref[tap:tap + 1, c0:c1] * shifted[tap * CHUNK:(tap + 1) * CHUNK, :]
        return _silu(conv)

    bc = conv_silu(D_SSM, D_XBC)
    tail[:, 0:D_SSM] = xs_ref[CHUNK - CONV_TAIL:CHUNK, :]
    tail[:, D_SSM:D_XBC] = bc_ref[CHUNK - CONV_TAIL:CHUNK, :]

    first_head = _lane_is_first_head()
    a_cs = acs_ref[...]
    w_p = wp_ref[...]
    src_t = src_t_ref[...]
    ri = lax.broadcasted_iota(jnp.int32, (CHUNK, CHUNK), 0)
    ci = lax.broadcasted_iota(jnp.int32, (CHUNK, CHUNK), 1)
    tril = ri >= ci

    def spread(v, pairs):
        return jnp.concatenate(
            [jnp.where(first_head, v[:, 2 * p:2 * p + 1], v[:, 2 * p + 1:2 * p + 2]) for p in pairs], axis=1)

    pairs_per_group = HEAD_PAIRS // SSM_GROUPS
    for g in range(SSM_GROUPS):
        pairs = range(g * pairs_per_group, (g + 1) * pairs_per_group)
        gcols = slice(pairs[0] * LANES, (pairs[-1] + 1) * LANES)
        xs = conv_silu(gcols.start, gcols.stop)
        xs_b = xs.astype(BF16)
        bg = bc[:, g * SSM_STATE:(g + 1) * SSM_STATE]
        cg = bc[:, (SSM_GROUPS + g) * SSM_STATE:(SSM_GROUPS + g + 1) * SSM_STATE].astype(BF16)
        cb = lax.dot_general(cg, bg.astype(BF16), (((1,), (1,)), ((), ())), preferred_element_type=F32)
        prev = state[:, gcols]
        y_off = jnp.dot(cg, prev.astype(BF16), preferred_element_type=F32)
        for i, p in enumerate(pairs):
            cols = slice(p * LANES, (p + 1) * LANES)
            local = slice(i * LANES, (i + 1) * LANES)
            xs_p = xs_b[:, local]
            zero = jnp.zeros_like(xs_p)
            m_pair = []
            for h in (2 * p, 2 * p + 1):
                seg = a_cs[:, h:h + 1] - src_t[h:h + 1, :]
                m_pair.append((cb * jnp.exp(jnp.where(tril, seg, NEG))).astype(BF16))
            rhs = jnp.concatenate([jnp.where(first_head, xs_p, zero), jnp.where(first_head, zero, xs_p)], axis=0)
            y = jnp.dot(jnp.concatenate(m_pair, axis=1), rhs, preferred_element_type=F32)
            y = y + y_off[:, local] * jnp.exp(spread(a_cs, [p])) + dskip_ref[:, cols] * xs[:, local]
            o_ref[:, cols] = y.astype(BF16)
        xw = (xs * spread(w_p, pairs)).astype(BF16)
        new = jnp.dot(bg.T.astype(BF16), xw, preferred_element_type=F32)
        state[:, gcols] = prev * jnp.exp(spread(a_cs[CHUNK - 1:CHUNK, :], pairs)) + new


def _ssd(proj3, src_t, a_cs, w_p, conv_w, conv_b, dskip):
    bsz, seq, _ = proj3.shape
    sel = np.zeros(((CONV_WIDTH - 1) * CHUNK, 2 * CHUNK), np.float32)
    for tap in range(CONV_WIDTH - 1):
        sel[tap * CHUNK + np.arange(CHUNK), CHUNK - (CONV_WIDTH - 1) + tap + np.arange(CHUNK)] = 1.0
    full = lambda shape: pl.BlockSpec(shape, lambda b, c: (0,) * len(shape))
    pos_major = pl.BlockSpec((None, CHUNK, LANES), lambda b, c: (b, c, 0))
    return pl.pallas_call(
        _ssd_body,
        grid=(bsz, seq // CHUNK),
        in_specs=[
            pl.BlockSpec((None, CHUNK, D_SSM), lambda b, c: (b, c, COL_XS)),
            pl.BlockSpec((None, CHUNK, D_BC), lambda b, c: (b, c, (COL_XS + 1) * D_SSM // D_BC)),
            pl.BlockSpec((None, None, SSM_HEADS, CHUNK), lambda b, c: (b, c, 0, 0)),
            pos_major, pos_major,
            full(sel.shape), full((CONV_WIDTH, D_XBC)), full((1, D_XBC)), full((1, D_SSM)),
        ],
        out_specs=pl.BlockSpec((None, CHUNK, D_SSM), lambda b, c: (b, c, 0)),
        out_shape=jax.ShapeDtypeStruct((bsz, seq, D_SSM), BF16),
        scratch_shapes=[
            pltpu.VMEM((2 * CHUNK, D_XBC), BF16),
            pltpu.VMEM((CONV_TAIL, D_XBC), BF16),
            pltpu.VMEM((SSM_STATE, D_SSM), F32),
        ],
        compiler_params=pltpu.CompilerParams(
            dimension_semantics=("parallel", "arbitrary"), vmem_limit_bytes=VMEM_LIMIT),
        name="ssd",
    )(proj3, proj3, src_t, a_cs, w_p, jnp.asarray(sel, BF16), conv_w, conv_b, dskip)


def _out_body(alpha, ya_ref, za_ref, ys_ref, zs_ref, x_ref, w_ref, ga_ref, gs_ref, lg_ref, lb_ref, o_ref):
    def gated_rms_norm(y_ref, gate_ref, g_ref):
        y = y_ref[...].astype(F32) * gate_ref[...].astype(F32)
        return (y * lax.rsqrt(jnp.mean(y * y, axis=-1, keepdims=True) + NORM_EPS) * g_ref[...]).astype(BF16)

    mix = jnp.concatenate([gated_rms_norm(ya_ref, za_ref, ga_ref), gated_rms_norm(ys_ref, zs_ref, gs_ref)], axis=1)
    h = alpha * x_ref[...] + jnp.dot(mix, w_ref[...], preferred_element_type=F32)
    mu = jnp.mean(h, axis=-1, keepdims=True)
    hc = h - mu
    var = jnp.mean(hc * hc, axis=-1, keepdims=True)
    o_ref[...] = hc * lax.rsqrt(var + NORM_EPS) * lg_ref[...] + lb_ref[...]


def _out_proj(alpha, y_att, proj2, y_ssm, x2d, w_out, gain_att, gain_ssm, ln_g, ln_b):
    m = x2d.shape[0]
    tm = OUT_ROWS
    rows = lambda shape: pl.BlockSpec(shape, lambda i: (i, 0))
    full = lambda shape: pl.BlockSpec(shape, lambda i: (0, 0))
    return pl.pallas_call(
        functools.partial(_out_body, alpha),
        grid=(m // tm,),
        in_specs=[
            rows((tm, D_ATT)),
            pl.BlockSpec((tm, D_ATT), lambda i: (i, COL_Z_ATT)),
            rows((tm, D_SSM)),
            pl.BlockSpec((tm, D_SSM), lambda i: (i, COL_Z_SSM)),
            rows((tm, D_MODEL)),
            full((D_ATT + D_SSM, D_MODEL)),
            full((1, D_ATT)), full((1, D_SSM)), full((1, D_MODEL)), full((1, D_MODEL)),
        ],
        out_specs=rows((tm, D_MODEL)),
        out_shape=jax.ShapeDtypeStruct((m, D_MODEL), F32),
        compiler_params=pltpu.CompilerParams(
            dimension_semantics=("parallel",), vmem_limit_bytes=VMEM_LIMIT),
        name="out_proj",
    )(y_att, proj2, y_ssm, proj2, x2d, w_out, gain_att, gain_ssm, ln_g, ln_b)


def kernel(x, w_in, conv_w, conv_b, dt_bias, a_log, d_skip, att_norm_g, ssm_norm_g, w_out, ln_g, ln_b):
    bsz, seq, _ = x.shape
    depth = w_in.shape[0]
    assert seq % UNIT == 0
    alpha = (2.0 * depth) ** 0.25
    slopes = jnp.asarray(2.0 ** (-8.0 * np.arange(1, ATT_HEADS + 1) / ATT_HEADS), dtype=F32)
    for layer in range(depth):
        w_main = w_in[layer][:, :D_PROJ].astype(BF16)
        w_dt = jnp.pad(w_in[layer][:, D_PROJ:], ((0, 0), (0, LANES - SSM_HEADS))).astype(BF16)
        x2d = x.reshape(bsz * seq, D_MODEL)
        proj, dt_raw = _in_proj(x2d, w_main, w_dt)
        proj3 = proj.reshape(bsz, seq, D_PROJ)

        y_att = _attention(proj3, slopes)

        src_t, a_cs, w_p = _ssd_decay(
            dt_raw.reshape(bsz, seq, LANES),
            dt_bias[layer].astype(F32).reshape(SSM_HEADS, 1), a_log[layer].astype(F32).reshape(SSM_HEADS, 1))
        y_ssm = _ssd(
            proj3, src_t, a_cs, w_p, conv_w[layer].astype(F32), conv_b[layer].astype(F32).reshape(1, D_XBC),
            jnp.repeat(d_skip[layer].astype(F32), HEAD_DIM).reshape(1, D_SSM))

        out = _out_proj(
            alpha, y_att.reshape(bsz * seq, D_ATT), proj, y_ssm.reshape(bsz * seq, D_SSM), x2d,
            w_out[layer].astype(BF16), att_norm_g[layer].astype(F32).reshape(1, D_ATT),
            ssm_norm_g[layer].astype(F32).reshape(1, D_SSM),
            ln_g[layer].astype(F32).reshape(1, D_MODEL), ln_b[layer].astype(F32).reshape(1, D_MODEL))
        x = out.reshape(bsz, seq, D_MODEL)
    return x
```

```python
import functools

import numpy as np
import jax
import jax.numpy as jnp
from jax import lax
from jax.experimental import pallas as pl
from jax.experimental.pallas import tpu as pltpu

F32 = jnp.float32
BF16 = jnp.bfloat16

D_MODEL = 1024
D_ATT = 1024
HEAD_DIM = 64
ATT_HEADS = D_ATT // HEAD_DIM
ATT_STEPS = 128
DILATIONS = (1, 4, 16)
D_SSM = 1024
SSM_HEADS = D_SSM // HEAD_DIM
SSM_GROUPS = 2
SSM_STATE = 128
CONV_WIDTH = 4
CHUNK = 128
D_BC = 2 * SSM_GROUPS * SSM_STATE
D_PROJ = 4 * D_ATT + D_SSM + D_SSM + D_BC
D_XBC = D_SSM + D_BC
NORM_EPS = 1e-5

LANES = 128
HEAD_PAIRS = ATT_HEADS // 2
UNIT = DILATIONS[-1] * ATT_STEPS
WIDE = DILATIONS[-1]
PADDED_GROUP = WIDE + 4
PADDED_UNIT = UNIT // WIDE * PADDED_GROUP
MIN_DENOMINATOR = 2.0 ** -100
LOG2_E = 1.4426950408889634
Q_SCALE = HEAD_DIM ** -0.5 * LOG2_E
NEG = -1e30
PROJ_ROWS, PROJ_COLS = 2048, 512
CONV_TAIL = 16
OUT_ROWS = 1024
DECAY_CHUNKS = 16
VMEM_LIMIT = 56 * 1024 * 1024

COL_Z_ATT = 3
COL_Z_SSM = 4
COL_XS = 5


def _silu(v):
    h = 0.5 * v
    return h + h * jnp.tanh(h)


def _lane_is_first_head():
    return lax.broadcasted_iota(jnp.int32, (1, LANES), 1) < HEAD_DIM


def _in_proj_body(x_ref, w_ref, wdt_ref, proj_ref, dt_ref, xb_ref):
    @pl.when(pl.program_id(1) == 0)
    def _():
        xb_ref[...] = x_ref[...].astype(BF16)
        dt_ref[...] = jnp.dot(xb_ref[...], wdt_ref[...], preferred_element_type=F32)

    def project():
        return jnp.dot(xb_ref[...], w_ref[...], preferred_element_type=F32)

    j = pl.program_id(1)
    is_q = j < D_ATT // PROJ_COLS
    is_gate = jnp.logical_and(j >= COL_Z_ATT * D_ATT // PROJ_COLS, j < COL_XS * D_SSM // PROJ_COLS)

    @pl.when(is_q)
    def _():
        proj_ref[...] = (project() * Q_SCALE).astype(BF16)

    @pl.when(is_gate)
    def _():
        proj_ref[...] = _silu(project()).astype(BF16)

    @pl.when(jnp.logical_not(jnp.logical_or(is_q, is_gate)))
    def _():
        proj_ref[...] = project().astype(BF16)


def _in_proj(x2d, w_main, w_dt):
    m = x2d.shape[0]
    tm, tn = PROJ_ROWS, PROJ_COLS
    assert m % tm == 0
    return pl.pallas_call(
        _in_proj_body,
        grid=(m // tm, D_PROJ // tn),
        in_specs=[
            pl.BlockSpec((tm, D_MODEL), lambda i, j: (jnp.minimum(i + jnp.minimum(j, 1), m // tm - 1), 0)),
            pl.BlockSpec((D_MODEL, tn), lambda i, j: (0, j)),
            pl.BlockSpec((D_MODEL, LANES), lambda i, j: (0, 0)),
        ],
        out_specs=[
            pl.BlockSpec((tm, tn), lambda i, j: (i, j)),
            pl.BlockSpec((tm, LANES), lambda i, j: (i, 0)),
        ],
        out_shape=[
            jax.ShapeDtypeStruct((m, D_PROJ), BF16),
            jax.ShapeDtypeStruct((m, LANES), F32),
        ],
        scratch_shapes=[pltpu.VMEM((tm, D_MODEL), BF16)],
        compiler_params=pltpu.CompilerParams(
            dimension_semantics=("parallel", "arbitrary"), vmem_limit_bytes=VMEM_LIMIT),
        name="in_proj",
    )(x2d, w_main, w_dt)


def _attn_body(slope_ref, ratio_ref, q_ref, k_ref, v_ref, o_ref, qf, kf, vf, qp, kp, vp, acc, m_s, l_s, bias):
    hp, b, u = pl.program_id(0), pl.program_id(1), pl.program_id(2)
    first_head = _lane_is_first_head()
    cur = (u + 1) * UNIT
    cur_padded = (u + 1) * PADDED_UNIT

    @pl.when(jnp.logical_and(b == 0, u == 0))
    def _():
        row = lax.broadcasted_iota(jnp.int32, (ATT_STEPS, 2 * ATT_STEPS), 0)
        col = lax.broadcasted_iota(jnp.int32, (ATT_STEPS, 2 * ATT_STEPS), 1)
        step = row + ATT_STEPS - col
        valid = (step >= 0) & (step <= ATT_STEPS)
        for di, d in enumerate(DILATIONS):
            table = jnp.where(valid, -(slope_ref[2 * hp] * LOG2_E) * (step * d).astype(F32), NEG)
            bias[2 * di] = table
            bias[2 * di + 1] = jnp.where(col < ATT_STEPS, NEG, table)

    @pl.when(jnp.logical_and(hp == 0, jnp.logical_and(b == 0, u == 0)))
    def _():
        kf[0:UNIT, :] = jnp.zeros((UNIT, LANES), F32)
        vf[0:UNIT, :] = jnp.zeros((UNIT, LANES), F32)
        kp[0:PADDED_UNIT, :] = jnp.zeros((PADDED_UNIT, LANES), F32)
        vp[0:PADDED_UNIT, :] = jnp.zeros((PADDED_UNIT, LANES), F32)

    q32, k32, v32 = q_ref[...].astype(F32), k_ref[...].astype(F32), v_ref[...].astype(F32)
    qf[...] = q32
    kf[pl.ds(cur, UNIT), :] = k32
    vf[pl.ds(cur, UNIT), :] = v32
    for g in range(UNIT // WIDE):
        group = slice(g * WIDE, (g + 1) * WIDE)
        qp[g * PADDED_GROUP:g * PADDED_GROUP + WIDE, :] = q32[group, :]
        kp[pl.ds(cur_padded + g * PADDED_GROUP, WIDE), :] = k32[group, :]
        vp[pl.ds(cur_padded + g * PADDED_GROUP, WIDE), :] = v32[group, :]

    def rows(d, start, n):
        return pl.ds(start, n) if d == 1 else pl.ds(start, n, stride=d)

    def stack_heads(q, zero):
        return jnp.concatenate([jnp.where(first_head, q, zero), jnp.where(first_head, zero, q)], axis=0)

    def load_block(d, qs):
        if d == 1:
            def reaching_back(x_ref, xf):
                before = xf[pl.ds(cur - ATT_STEPS, ATT_STEPS), :].astype(BF16)
                return jnp.concatenate([before, x_ref[0:ATT_STEPS, :]], axis=0)

            if isinstance(qs, int):
                q = q_ref[qs:qs + ATT_STEPS, :]
                if qs == 0:
                    kk, vv = reaching_back(k_ref, kf), reaching_back(v_ref, vf)
                else:
                    kk, vv = (x_ref[qs - ATT_STEPS:qs + ATT_STEPS, :] for x_ref in (k_ref, v_ref))
            else:
                q = q_ref[pl.ds(pl.multiple_of(qs, ATT_STEPS), ATT_STEPS), :]
                inside = pl.ds(pl.multiple_of(jnp.maximum(qs - ATT_STEPS, 0), ATT_STEPS), 2 * ATT_STEPS)
                kk = jnp.where(qs == 0, reaching_back(k_ref, kf), k_ref[inside, :])
                vv = jnp.where(qs == 0, reaching_back(v_ref, vf), v_ref[inside, :])
            return stack_heads(q, jnp.zeros_like(q)), kk, vv
        if d == WIDE:
            window = pl.ds(cur_padded - PADDED_UNIT + qs, 2 * ATT_STEPS, stride=PADDED_GROUP)
            q2 = stack_heads(qp[pl.ds(qs, ATT_STEPS, stride=PADDED_GROUP), :], 0.0).astype(BF16)
            return q2, kp[window, :].astype(BF16), vp[window, :].astype(BF16)
        window = rows(d, cur + qs - ATT_STEPS * d, 2 * ATT_STEPS)
        q2 = stack_heads(qf[rows(d, qs, ATT_STEPS), :], 0.0).astype(BF16)
        return q2, kf[window, :].astype(BF16), vf[window, :].astype(BF16)

    def block_scores(di, d, qs, first):
        q2, kk, vv = load_block(d, qs)
        s = lax.dot_general(q2, kk, (((1,), (1,)), ((), ())), preferred_element_type=F32)
        table = bias[2 * di + first.astype(jnp.int32)]
        return s + jnp.concatenate([table, table * ratio_ref[hp]], axis=0), vv

    def weighted_values(p, vv):
        vext = jnp.concatenate([vv, jnp.ones_like(vv)], axis=1)
        oe = jnp.dot(p, vext, preferred_element_type=F32)
        o_blk = jnp.where(first_head, oe[:ATT_STEPS, :LANES], oe[ATT_STEPS:, :LANES])
        l_blk = jnp.where(first_head, oe[:ATT_STEPS, LANES:], oe[ATT_STEPS:, LANES:])
        return o_blk, l_blk

    def block_start(d, t):
        blk = t // d
        r = t - blk * d
        return blk * (ATT_STEPS * d) + r, jnp.logical_and(u == 0, blk == 0)

    n_sub = UNIT // ATT_STEPS
    order = tuple(reversed(range(len(DILATIONS))))

    for di in order:
        d = DILATIONS[di]
        for t in range(n_sub):
            qs, first = block_start(d, t)
            s, vv = block_scores(di, d, qs, first)
            o_blk, l_blk = weighted_values(jnp.exp2(s).astype(BF16), vv)
            rows_q = rows(d, qs, ATT_STEPS)
            if di == order[0]:
                acc[rows_q, :] = o_blk
                l_s[rows_q, :] = l_blk
            else:
                o_blk = acc[rows_q, :] + o_blk
                l_blk = l_s[rows_q, :] + l_blk
                acc[rows_q, :] = o_blk
                l_s[rows_q, :] = l_blk
                if di == order[-1]:
                    o_ref[rows_q, :] = (o_blk / l_blk).astype(BF16)

    den = l_s[...]
    in_range = (den >= MIN_DENOMINATOR) & (den < jnp.inf) & (jnp.abs(acc[...]) < jnp.inf)
    n_bad = jnp.max(jnp.max(jnp.where(in_range, 0.0, 1.0), axis=1, keepdims=True), axis=0, keepdims=True)

    @pl.when(n_bad[0, 0] > 0.0)
    def _():
        for di in order:
            d = DILATIONS[di]

            def body(t, carry, di=di, d=d):
                qs, first = block_start(d, t)
                s, vv = block_scores(di, d, qs, first)
                mb = jnp.max(s, axis=1, keepdims=True)
                o_blk, l_blk = weighted_values(jnp.exp2(s - mb).astype(BF16), vv)
                m_blk = jnp.where(first_head, mb[:ATT_STEPS], mb[ATT_STEPS:])
                rows_q = rows(d, qs, ATT_STEPS)
                if di == order[0]:
                    acc[rows_q, :] = o_blk
                    l_s[rows_q, :] = l_blk
                    m_s[rows_q, :] = m_blk
                else:
                    m_old = m_s[rows_q, :]
                    m_new = jnp.maximum(m_old, m_blk)
                    c_old = jnp.exp2(m_old - m_new)
                    c_blk = jnp.exp2(m_blk - m_new)
                    acc[rows_q, :] = acc[rows_q, :] * c_old + o_blk * c_blk
                    l_s[rows_q, :] = l_s[rows_q, :] * c_old + l_blk * c_blk
                    m_s[rows_q, :] = m_new
                return carry
            lax.fori_loop(0, n_sub, body, 0)
        o_ref[...] = (acc[...] / l_s[...]).astype(BF16)


def _attention(proj3, slopes):
    bsz, seq, _ = proj3.shape
    n_units = seq // UNIT
    blk = (None, UNIT, LANES)
    section = lambda k: pl.BlockSpec(blk, lambda h, b, u: (b, u, k * HEAD_PAIRS + h))
    return pl.pallas_call(
        _attn_body,
        grid=(HEAD_PAIRS, bsz, n_units),
        in_specs=[pl.BlockSpec(memory_space=pltpu.SMEM), pl.BlockSpec(memory_space=pltpu.SMEM),
                  section(0), section(1), section(2)],
        out_specs=section(0),
        out_shape=jax.ShapeDtypeStruct((bsz, seq, D_ATT), BF16),
        scratch_shapes=[
            pltpu.VMEM((UNIT, LANES), F32),
            pltpu.VMEM(((n_units + 1) * UNIT, LANES), F32),
            pltpu.VMEM(((n_units + 1) * UNIT, LANES), F32),
            pltpu.VMEM((PADDED_UNIT, LANES), F32),
            pltpu.VMEM(((n_units + 1) * PADDED_UNIT, LANES), F32),
            pltpu.VMEM(((n_units + 1) * PADDED_UNIT, LANES), F32),
            pltpu.VMEM((UNIT, LANES), F32),
            pltpu.VMEM((UNIT, LANES), F32),
            pltpu.VMEM((UNIT, LANES), F32),
            pltpu.VMEM((2 * len(DILATIONS), ATT_STEPS, 2 * ATT_STEPS), F32),
        ],
        compiler_params=pltpu.CompilerParams(
            dimension_semantics=("arbitrary", "arbitrary", "arbitrary"), vmem_limit_bytes=VMEM_LIMIT),
        name="dilated_attention",
    )(slopes, slopes[1::2] / slopes[0::2], proj3, proj3, proj3)


def _split_bf16(v, parts):
    out = []
    for _ in range(parts):
        hi = v.astype(BF16)
        out.append(hi)
        v = v - hi.astype(F32)
    return out


def _ssd_decay_body(dt_ref, dtb_ref, alog_ref, src_t_ref, acs_ref, w_ref):
    ri = lax.broadcasted_iota(jnp.int32, (CHUNK, CHUNK), 0)
    ci = lax.broadcasted_iota(jnp.int32, (CHUNK, CHUNK), 1)
    triu_b = jnp.where(ri <= ci, 1.0, 0.0).astype(BF16)
    pad_rows = jnp.zeros((CHUNK - SSM_HEADS, CHUNK), F32)
    chunk_rows = [slice(g * CHUNK, (g + 1) * CHUNK) for g in range(DECAY_CHUNKS)]
    per_chunk = lambda v: jnp.concatenate([v] * DECAY_CHUNKS, axis=0)
    dt_in = jnp.concatenate([dt_ref[r, :].T[0:SSM_HEADS, :] for r in chunk_rows], axis=0) + per_chunk(dtb_ref[...])
    dt_t = jnp.maximum(dt_in, 0.0) + jnp.log1p(jnp.exp(-jnp.abs(dt_in)))
    adt_t = dt_t * per_chunk(-jnp.exp(alog_ref[...]))
    a_cs_t = sum(jnp.dot(part, triu_b, preferred_element_type=F32) for part in _split_bf16(adt_t, 3))
    w_t = dt_t * jnp.exp(a_cs_t[:, CHUNK - 1:CHUNK] - a_cs_t)
    src_t = a_cs_t - jnp.log(dt_t)
    for g, r in enumerate(chunk_rows):
        heads = slice(g * SSM_HEADS, (g + 1) * SSM_HEADS)
        src_t_ref[g] = src_t[heads, :]
        acs_ref[r, :] = jnp.concatenate([a_cs_t[heads, :], pad_rows], axis=0).T
        w_ref[r, :] = jnp.concatenate([w_t[heads, :], pad_rows], axis=0).T


def _ssd_decay(dt3, dtb, alog):
    bsz, seq, _ = dt3.shape
    n_chunks = seq // CHUNK
    rows = DECAY_CHUNKS * CHUNK
    full = lambda shape: pl.BlockSpec(shape, lambda b, i: (0,) * len(shape))
    pos_major = pl.BlockSpec((None, rows, LANES), lambda b, i: (b, i, 0))
    return pl.pallas_call(
        _ssd_decay_body,
        grid=(bsz, n_chunks // DECAY_CHUNKS),
        in_specs=[pos_major, full((SSM_HEADS, 1)), full((SSM_HEADS, 1))],
        out_specs=[
            pl.BlockSpec((None, DECAY_CHUNKS, SSM_HEADS, CHUNK), lambda b, i: (b, i, 0, 0)),
            pos_major, pos_major,
        ],
        out_shape=[
            jax.ShapeDtypeStruct((bsz, n_chunks, SSM_HEADS, CHUNK), F32),
            jax.ShapeDtypeStruct((bsz, seq, LANES), F32),
            jax.ShapeDtypeStruct((bsz, seq, LANES), F32),
        ],
        compiler_params=pltpu.CompilerParams(
            dimension_semantics=("parallel", "parallel"), vmem_limit_bytes=VMEM_LIMIT),
        name="ssd_decay",
    )(dt3, dtb, alog)


def _ssd_body(xs_ref, bc_ref, src_t_ref, acs_ref, wp_ref, sel_ref, cw_ref, cb_ref, dskip_ref, o_ref,
              upad, tail, state):
    @pl.when(pl.program_id(1) == 0)
    def _():
        upad[0:CHUNK, :] = jnp.zeros((CHUNK, D_XBC), BF16)
        tail[...] = jnp.zeros_like(tail)
        state[...] = jnp.zeros_like(state)

    upad[CHUNK - CONV_TAIL:CHUNK, :] = tail[...]
    upad[CHUNK:2 * CHUNK, 0:D_SSM] = xs_ref[...]
    upad[CHUNK:2 * CHUNK, D_SSM:D_XBC] = bc_ref[...]

    def conv_silu(c0, c1):
        shifted = jnp.dot(sel_ref[...], upad[:, c0:c1], preferred_element_type=F32)
        conv = cb_ref[:, c0:c1] + cw_ref[CONV_WIDTH - 1:CONV_WIDTH, c0:c1] * upad[CHUNK:2 * CHUNK, c0:c1].astype(F32)
        for tap in range(CONV_WIDTH - 1):
            conv = conv + cw_ref[tap:tap + 1, c0:c1] * shifted[tap * CHUNK:(tap + 1) * CHUNK, :]
        return _silu(conv)

    bc = conv_silu(D_SSM, D_XBC)
    tail[:, 0:D_SSM] = xs_ref[CHUNK - CONV_TAIL:CHUNK, :]
    tail[:, D_SSM:D_XBC] = bc_ref[CHUNK - CONV_TAIL:CHUNK, :]

    first_head = _lane_is_first_head()
    a_cs = acs_ref[...]
    w_p = wp_ref[...]
    src_t = src_t_ref[...]
    ri = lax.broadcasted_iota(jnp.int32, (CHUNK, CHUNK), 0)
    ci = lax.broadcasted_iota(jnp.int32, (CHUNK, CHUNK), 1)
    tril = ri >= ci

    def spread(v, pairs):
        return jnp.concatenate(
            [jnp.where(first_head, v[:, 2 * p:2 * p + 1], v[:, 2 * p + 1:2 * p + 2]) for p in pairs], axis=1)

    pairs_per_group = HEAD_PAIRS // SSM_GROUPS
    for g in range(SSM_GROUPS):
        pairs = range(g * pairs_per_group, (g + 1) * pairs_per_group)
        gcols = slice(pairs[0] * LANES, (pairs[-1] + 1) * LANES)
        xs = conv_silu(gcols.start, gcols.stop)
        xs_b = xs.astype(BF16)
        bg = bc[:, g * SSM_STATE:(g + 1) * SSM_STATE]
        cg = bc[:, (SSM_GROUPS + g) * SSM_STATE:(SSM_GROUPS + g + 1) * SSM_STATE].astype(BF16)
        cb = lax.dot_general(cg, bg.astype(BF16), (((1,), (1,)), ((), ())), preferred_element_type=F32)
        prev = state[:, gcols]
        y_off = jnp.dot(cg, prev.astype(BF16), preferred_element_type=F32)
        for i, p in enumerate(pairs):
            cols = slice(p * LANES, (p + 1) * LANES)
            local = slice(i * LANES, (i + 1) * LANES)
            xs_p = xs_b[:, local]
            zero = jnp.zeros_like(xs_p)
            m_pair = []
            for h in (2 * p, 2 * p + 1):
                seg = a_cs[:, h:h + 1] - src_t[h:h + 1, :]
                m_pair.append((cb * jnp.exp(jnp.where(tril, seg, NEG))).astype(BF16))
            rhs = jnp.concatenate([jnp.where(first_head, xs_p, zero), jnp.where(first_head, zero, xs_p)], axis=0)
            y = jnp.dot(jnp.concatenate(m_pair, axis=1), rhs, preferred_element_type=F32)
            y = y + y_off[:, local] * jnp.exp(spread(a_cs, [p])) + dskip_ref[:, cols] * xs[:, local]
            o_ref[:, cols] = y.astype(BF16)
        xw = (xs * spread(w_p, pairs)).astype(BF16)
        new = jnp.dot(bg.T.astype(BF16), xw, preferred_element_type=F32)
        state[:, gcols] = prev * jnp.exp(spread(a_cs[CHUNK - 1:CHUNK, :], pairs)) + new


def _ssd(proj3, src_t, a_cs, w_p, conv_w, conv_b, dskip):
    bsz, seq, _ = proj3.shape
    sel = np.zeros(((CONV_WIDTH - 1) * CHUNK, 2 * CHUNK), np.float32)
    for tap in range(CONV_WIDTH - 1):
        sel[tap * CHUNK + np.arange(CHUNK), CHUNK - (CONV_WIDTH - 1) + tap + np.arange(CHUNK)] = 1.0
    full = lambda shape: pl.BlockSpec(shape, lambda b, c: (0,) * len(shape))
    pos_major = pl.BlockSpec((None, CHUNK, LANES), lambda b, c: (b, c, 0))
    return pl.pallas_call(
        _ssd_body,
        grid=(bsz, seq // CHUNK),
        in_specs=[
            pl.BlockSpec((None, CHUNK, D_SSM), lambda b, c: (b, c, COL_XS)),
            pl.BlockSpec((None, CHUNK, D_BC), lambda b, c: (b, c, (COL_XS + 1) * D_SSM // D_BC)),
            pl.BlockSpec((None, None, SSM_HEADS, CHUNK), lambda b, c: (b, c, 0, 0)),
            pos_major, pos_major,
            full(sel.shape), full((CONV_WIDTH, D_XBC)), full((1, D_XBC)), full((1, D_SSM)),
        ],
        out_specs=pl.BlockSpec((None, CHUNK, D_SSM), lambda b, c: (b, c, 0)),
        out_shape=jax.ShapeDtypeStruct((bsz, seq, D_SSM), BF16),
        scratch_shapes=[
            pltpu.VMEM((2 * CHUNK, D_XBC), BF16),
            pltpu.VMEM((CONV_TAIL, D_XBC), BF16),
            pltpu.VMEM((SSM_STATE, D_SSM), F32),
        ],
        compiler_params=pltpu.CompilerParams(
            dimension_semantics=("parallel", "arbitrary"), vmem_limit_bytes=VMEM_LIMIT),
        name="ssd",
    )(proj3, proj3, src_t, a_cs, w_p, jnp.asarray(sel, BF16), conv_w, conv_b, dskip)


def _out_body(alpha, ya_ref, za_ref, ys_ref, zs_ref, x_ref, w_ref, ga_ref, gs_ref, lg_ref, lb_ref, o_ref):
    def gated_rms_norm(y_ref, gate_ref, g_ref):
        y = y_ref[...].astype(F32) * gate_ref[...].astype(F32)
        return (y * lax.rsqrt(jnp.mean(y * y, axis=-1, keepdims=True) + NORM_EPS) * g_ref[...]).astype(BF16)

    mix = jnp.concatenate([gated_rms_norm(ya_ref, za_ref, ga_ref), gated_rms_norm(ys_ref, zs_ref, gs_ref)], axis=1)
    h = alpha * x_ref[...] + jnp.dot(mix, w_ref[...], preferred_element_type=F32)
    mu = jnp.mean(h, axis=-1, keepdims=True)
    hc = h - mu
    var = jnp.mean(hc * hc, axis=-1, keepdims=True)
    o_ref[...] = hc * lax.rsqrt(var + NORM_EPS) * lg_ref[...] + lb_ref[...]


def _out_proj(alpha, y_att, proj2, y_ssm, x2d, w_out, gain_att, gain_ssm, ln_g, ln_b):
    m = x2d.shape[0]
    tm = OUT_ROWS
    rows = lambda shape: pl.BlockSpec(shape, lambda i: (i, 0))
    full = lambda shape: pl.BlockSpec(shape, lambda i: (0, 0))
    return pl.pallas_call(
        functools.partial(_out_body, alpha),
        grid=(m // tm,),
        in_specs=[
            rows((tm, D_ATT)),
            pl.BlockSpec((tm, D_ATT), lambda i: (i, COL_Z_ATT)),
            rows((tm, D_SSM)),
            pl.BlockSpec((tm, D_SSM), lambda i: (i, COL_Z_SSM)),
            rows((tm, D_MODEL)),
            full((D_ATT + D_SSM, D_MODEL)),
            full((1, D_ATT)), full((1, D_SSM)), full((1, D_MODEL)), full((1, D_MODEL)),
        ],
        out_specs=rows((tm, D_MODEL)),
        out_shape=jax.ShapeDtypeStruct((m, D_MODEL), F32),
        compiler_params=pltpu.CompilerParams(
            dimension_semantics=("parallel",), vmem_limit_bytes=VMEM_LIMIT),
        name="out_proj",
    )(y_att, proj2, y_ssm, proj2, x2d, w_out, gain_att, gain_ssm, ln_g, ln_b)


def kernel(x, w_in, conv_w, conv_b, dt_bias, a_log, d_skip, att_norm_g, ssm_norm_g, w_out, ln_g, ln_b):
    bsz, seq, _ = x.shape
    depth = w_in.shape[0]
    assert seq % UNIT == 0
    alpha = (2.0 * depth) ** 0.25
    slopes = jnp.asarray(2.0 ** (-8.0 * np.arange(1, ATT_HEADS + 1) / ATT_HEADS), dtype=F32)
    for layer in range(depth):
        w_main = w_in[layer][:, :D_PROJ].astype(BF16)
        w_dt = jnp.pad(w_in[layer][:, D_PROJ:], ((0, 0), (0, LANES - SSM_HEADS))).astype(BF16)
        x2d = x.reshape(bsz * seq, D_MODEL)
        proj, dt_raw = _in_proj(x2d, w_main, w_dt)
        proj3 = proj.reshape(bsz, seq, D_PROJ)

        y_att = _attention(proj3, slopes)

        src_t, a_cs, w_p = _ssd_decay(
            dt_raw.reshape(bsz, seq, LANES),
            dt_bias[layer].astype(F32).reshape(SSM_HEADS, 1), a_log[layer].astype(F32).reshape(SSM_HEADS, 1))
        y_ssm = _ssd(
            proj3, src_t, a_cs, w_p, conv_w[layer].astype(F32), conv_b[layer].astype(F32).reshape(1, D_XBC),
            jnp.repeat(d_skip[layer].astype(F32), HEAD_DIM).reshape(1, D_SSM))

        out = _out_proj(
            alpha, y_att.reshape(bsz * seq, D_ATT), proj, y_ssm.reshape(bsz * seq, D_SSM), x2d,
            w_out[layer].astype(BF16), att_norm_g[layer].astype(F32).reshape(1, D_ATT),
            ssm_norm_g[layer].astype(F32).reshape(1, D_SSM),
            ln_g[layer].astype(F32).reshape(1, D_MODEL), ln_b[layer].astype(F32).reshape(1, D_MODEL))
        x = out.reshape(bsz, seq, D_MODEL)
    return x
```
